```python
import math
import jax, jax.numpy as jnp
from jax import lax
import numpy as np

D_MODEL = 1024
BATCH = 16
SEQ = 2048
DEPTH = 4

CHUNK = 64
N_MIXERS = 2

MLA_HEADS = 8
MLA_Q_LORA = 512
MLA_KV_LORA = 256
MLA_NOPE = 128
MLA_ROPE = 64
MLA_V = 128
ROPE_THETA = 10000.0
Q_BLOCK = 128

ML_HEADS = 4
ML_DV = D_MODEL // ML_HEADS
ML_DK = ML_DV // 2

D_FF = 2816
CONV_W = 3

EPS = 1e-6
N_MLA_LAYERS = (DEPTH + N_MIXERS - 1) // N_MIXERS
N_ML_LAYERS = DEPTH // N_MIXERS

kernel_name = "hybrid_mla_mlstm_convffn_adaln"


def rms_norm(x, gain=None):
    x32 = x.astype(jnp.float32)
    y = x32 * lax.rsqrt(jnp.mean(x32 * x32, axis=-1, keepdims=True) + EPS)
    if gain is not None:
        y = y * gain.astype(jnp.float32)
    return y.astype(x.dtype)


def rope_tables(positions):
    inv_freq = 1.0 / (ROPE_THETA ** (jnp.arange(0, MLA_ROPE, 2, dtype=jnp.float32) / MLA_ROPE))
    ang = positions.astype(jnp.float32)[..., None] * inv_freq
    return jnp.cos(ang), jnp.sin(ang)


def apply_rope(t, cos, sin):
    t32 = t.astype(jnp.float32)
    t1, t2 = jnp.split(t32, 2, axis=-1)
    out = jnp.concatenate([t1 * cos - t2 * sin, t1 * sin + t2 * cos], axis=-1)
    return out.astype(t.dtype)


def mla_mixer(h, cos, sin, w_in, q_norm, w_q_up, kv_norm, w_kv_up, w_out):
    B, S, _ = h.shape
    proj = h @ w_in
    cq, ckv, k_rope = jnp.split(proj, [MLA_Q_LORA, MLA_Q_LORA + MLA_KV_LORA], axis=-1)
    q = (rms_norm(cq, q_norm) @ w_q_up).reshape(B, S, MLA_HEADS, MLA_NOPE + MLA_ROPE)
    q_nope = q[..., :MLA_NOPE]
    q_rope = apply_rope(q[..., MLA_NOPE:], cos[:, :, None, :], sin[:, :, None, :])
    k_rope = apply_rope(k_rope, cos, sin)
    kv = (rms_norm(ckv, kv_norm) @ w_kv_up).reshape(B, S, MLA_HEADS, MLA_NOPE + MLA_V)
    k_nope, v = kv[..., :MLA_NOPE], kv[..., MLA_NOPE:]
    scale = (MLA_NOPE + MLA_ROPE) ** -0.5
    n_blk = S // Q_BLOCK
    k_chunk = jnp.arange(S) // CHUNK

    def to_blocks(t):
        return t.reshape(B, n_blk, Q_BLOCK, *t.shape[2:]).swapaxes(0, 1)

    def attend(args):
        qn, qr, blk = args
        s = (jnp.einsum('bqhd,bkhd->bhqk', qn, k_nope)
             + jnp.einsum('bqhr,bkr->bhqk', qr, k_rope)).astype(jnp.float32) * scale
        q_chunk = (blk * Q_BLOCK + jnp.arange(Q_BLOCK)) // CHUNK
        allowed = k_chunk[None, :] <= q_chunk[:, None]
        s = jnp.where(allowed, s, -jnp.inf)
        p = jax.nn.softmax(s, axis=-1).astype(v.dtype)
        return jnp.einsum('bhqk,bkhd->bqhd', p, v)

    o = lax.map(attend, (to_blocks(q_nope), to_blocks(q_rope), jnp.arange(n_blk)))
    o = o.swapaxes(0, 1).reshape(B, S, MLA_HEADS * MLA_V)
    return o @ w_out


def mlstm_mixer(h, w_in, b_gates, head_norm, w_out):
    B, S, _ = h.shape
    NC = S // CHUNK
    H = ML_HEADS
    qk_w, v_w = H * ML_DK, H * ML_DV
    proj = h @ w_in
    q, k, v, o, gates = jnp.split(proj, [qk_w, 2 * qk_w, 2 * qk_w + v_w, 2 * qk_w + 2 * v_w], axis=-1)
    gates = gates.astype(jnp.float32) + b_gates.astype(jnp.float32)
    i_pre, f_pre = gates[..., :H], gates[..., H:]
    log_f = jax.nn.log_sigmoid(f_pre)

    def to_chunks(t, d):
        return t.astype(jnp.float32).reshape(B, NC, CHUNK, H, d).transpose(1, 0, 3, 2, 4)

    def gate_chunks(t):
        return t.reshape(B, NC, CHUNK, H).transpose(1, 0, 3, 2)

    qc = to_chunks(q, ML_DK)
    kc = to_chunks(k, ML_DK) * (ML_DK ** -0.5)
    vc = to_chunks(v, ML_DV)
    ic = gate_chunks(i_pre)
    bc = lax.cumsum(gate_chunks(log_f), axis=3)
    causal = jnp.tril(jnp.ones((CHUNK, CHUNK), dtype=bool))

    def step(carry, inp):
        C, n, m = carry
        qt, kt, vt, it, bt = inp
        dmat = jnp.where(causal, bt[..., :, None] - bt[..., None, :] + it[..., None, :], -jnp.inf)
        inter_log = bt + m[..., None]
        m_t = jnp.maximum(inter_log, jnp.max(dmat, axis=-1))
        inter_w = jnp.exp(inter_log - m_t)
        s_mat = jnp.einsum('bhtd,bhsd->bhts', qt, kt) * jnp.exp(dmat - m_t[..., None])
        num = (inter_w[..., None] * jnp.einsum('bhvd,bhtd->bhtv', C, qt)
               + jnp.einsum('bhts,bhsv->bhtv', s_mat, vt))
        den = inter_w * jnp.einsum('bhd,bhtd->bht', n, qt) + jnp.sum(s_mat, axis=-1)
        h_out = num / jnp.maximum(jnp.abs(den), jnp.exp(-m_t))[..., None]
        b_last = bt[..., -1]
        w_log = b_last[..., None] - bt + it
        m_new = jnp.maximum(b_last + m, jnp.max(w_log, axis=-1))
        decay = jnp.exp(b_last + m - m_new)
        ws = jnp.exp(w_log - m_new[..., None])
        C_new = decay[..., None, None] * C + jnp.einsum('bhs,bhsv,bhsd->bhvd', ws, vt, kt)
        n_new = decay[..., None] * n + jnp.einsum('bhs,bhsd->bhd', ws, kt)
        return (C_new, n_new, m_new), h_out

    init = (jnp.zeros((B, H, ML_DV, ML_DK), jnp.float32),
            jnp.zeros((B, H, ML_DK), jnp.float32),
            jnp.zeros((B, H), jnp.float32))
    _, hc = lax.scan(step, init, (qc, kc, vc, ic, bc))
    hs = hc.transpose(1, 0, 3, 2, 4).reshape(B, S, H, ML_DV)
    hs = hs * lax.rsqrt(jnp.mean(hs * hs, axis=-1, keepdims=True) + EPS)
    hs = hs * head_norm.astype(jnp.float32).reshape(H, ML_DV)
    y = hs.reshape(B, S, v_w) * jax.nn.sigmoid(o.astype(jnp.float32))
    return y.astype(h.dtype) @ w_out


def conv_ffn(h, w_up, conv_w, conv_b, w_down):
    a, g = jnp.split(h @ w_up, 2, axis=-1)
    a = lax.conv_general_dilated(a, conv_w[:, None, :], window_strides=(1,),
                                 padding=[(CONV_W - 1, 0)],
                                 dimension_numbers=('NWC', 'WIO', 'NWC'),
                                 feature_group_count=D_FF) + conv_b
    return (jax.nn.gelu(a, approximate=False) * g) @ w_down


def setup_inputs(seed: int = 0) -> dict:
    key = jax.random.key(seed)
    ks = iter(jax.random.split(key, 32))
    D = D_MODEL

    def nrm(shape, fan_in, mult=1.0):
        return jax.random.normal(next(ks), shape, jnp.float32) * (mult * fan_in ** -0.5)

    def gain(shape):
        return 1.0 + 0.02 * jax.random.normal(next(ks), shape, jnp.float32)

    x = jax.random.normal(next(ks), (BATCH, SEQ, D), jnp.float32)
    c = jax.random.normal(next(ks), (BATCH, D), jnp.float32)
    offset = jax.random.randint(next(ks), (BATCH,), 0, 4096, dtype=jnp.int32)
    positions = offset[:, None] + jnp.arange(SEQ, dtype=jnp.int32)[None, :]

    mod_w = nrm((DEPTH, D, 6 * D), D, 0.5)
    mod_b = 0.02 * jax.random.normal(next(ks), (DEPTH, 6 * D), jnp.float32)

    NA = N_MLA_LAYERS
    mla_w_in = nrm((NA, D, MLA_Q_LORA + MLA_KV_LORA + MLA_ROPE), D)
    mla_q_norm = gain((NA, MLA_Q_LORA))
    mla_w_q_up = nrm((NA, MLA_Q_LORA, MLA_HEADS * (MLA_NOPE + MLA_ROPE)), MLA_Q_LORA)
    mla_kv_norm = gain((NA, MLA_KV_LORA))
    mla_w_kv_up = nrm((NA, MLA_KV_LORA, MLA_HEADS * (MLA_NOPE + MLA_V)), MLA_KV_LORA)
    mla_w_out = nrm((NA, MLA_HEADS * MLA_V, D), MLA_HEADS * MLA_V)

    NB = N_ML_LAYERS
    ml_w_in = nrm((NB, D, 2 * ML_HEADS * ML_DK + 2 * ML_HEADS * ML_DV + 2 * ML_HEADS), D)
    i_bias = 0.1 * jax.random.normal(next(ks), (NB, ML_HEADS), jnp.float32)
    f_bias = (jnp.linspace(3.0, 6.0, ML_HEADS, dtype=jnp.float32)[None, :]
              + 0.1 * jax.random.normal(next(ks), (NB, ML_HEADS), jnp.float32))
    ml_b_gates = jnp.concatenate([i_bias, f_bias], axis=-1)
    ml_head_norm = gain((NB, ML_HEADS * ML_DV))
    ml_w_out = nrm((NB, ML_HEADS * ML_DV, D), ML_HEADS * ML_DV)

    ffn_w_up = nrm((DEPTH, D, 2 * D_FF), D)
    ffn_conv_w = nrm((DEPTH, CONV_W, D_FF), CONV_W)
    ffn_conv_b = 0.02 * jax.random.normal(next(ks), (DEPTH, D_FF), jnp.float32)
    ffn_w_down = nrm((DEPTH, D_FF, D), D_FF)

    final_norm = gain((D,))

    return {"x": x, "c": c, "positions": positions,
            "mod_w": mod_w, "mod_b": mod_b,
            "mla_w_in": mla_w_in, "mla_q_norm": mla_q_norm, "mla_w_q_up": mla_w_q_up,
            "mla_kv_norm": mla_kv_norm, "mla_w_kv_up": mla_w_kv_up, "mla_w_out": mla_w_out,
            "ml_w_in": ml_w_in, "ml_b_gates": ml_b_gates, "ml_head_norm": ml_head_norm,
            "ml_w_out": ml_w_out,
            "ffn_w_up": ffn_w_up, "ffn_conv_w": ffn_conv_w, "ffn_conv_b": ffn_conv_b,
            "ffn_w_down": ffn_w_down, "final_norm": final_norm}


def reference(x, c, positions, mod_w, mod_b,
              mla_w_in, mla_q_norm, mla_w_q_up, mla_kv_norm, mla_w_kv_up, mla_w_out,
              ml_w_in, ml_b_gates, ml_head_norm, ml_w_out,
              ffn_w_up, ffn_conv_w, ffn_conv_b, ffn_w_down, final_norm):
    cos, sin = rope_tables(positions)
    c_act = jax.nn.silu(c)
    for i in range(DEPTH):
        mod = c_act @ mod_w[i] + mod_b[i]
        sh_a, sc_a, g_a, sh_f, sc_f, g_f = [t[:, None, :] for t in jnp.split(mod, 6, axis=-1)]
        h = rms_norm(x) * (1 + sc_a) + sh_a
        j = i // N_MIXERS
        if i % N_MIXERS == 0:
            y = mla_mixer(h, cos, sin, mla_w_in[j], mla_q_norm[j], mla_w_q_up[j],
                          mla_kv_norm[j], mla_w_kv_up[j], mla_w_out[j])
        else:
            y = mlstm_mixer(h, ml_w_in[j], ml_b_gates[j], ml_head_norm[j], ml_w_out[j])
        x = x + g_a * y
        h = rms_norm(x) * (1 + sc_f) + sh_f
        x = x + g_f * conv_ffn(h, ffn_w_up[i], ffn_conv_w[i], ffn_conv_b[i], ffn_w_down[i])
    return rms_norm(x, final_norm)
```

```python
import functools
import math

import jax
import jax.numpy as jnp
from jax import lax
from jax.experimental import pallas as pl
from jax.experimental.pallas import tpu as pltpu

EPS = 1e-6
ROPE_THETA = 10000.0
CHUNK = 64
MLA_HEADS = 8
MLA_Q_LORA = 512
MLA_KV_LORA = 256
MLA_NOPE = 128
MLA_ROPE = 64
MLA_V = 128
ML_HEADS = 4
CONV_W = 3

LANES = 128
SUBLANES = 8
QK_PAD = 256
ML_CHUNK = 256
VMEM_LIMIT = 56 * 1024 * 1024

BF16 = jnp.bfloat16
F32 = jnp.float32


def _params(sem):
    return pltpu.CompilerParams(dimension_semantics=sem, vmem_limit_bytes=VMEM_LIMIT)


def _const_spec(shape):
    nd = len(shape)
    return pl.BlockSpec(shape, lambda *_: (0,) * nd, pipeline_mode=pl.Buffered(1))


def _rms(x):
    return x * lax.rsqrt(jnp.mean(x * x, axis=-1, keepdims=True) + EPS)


def _mod_kernel(c_ref, w_ref, b_ref, o_ref):
    c = c_ref[...]
    ca = (c * jax.nn.sigmoid(c)).astype(BF16)
    o_ref[0] = jnp.dot(ca, w_ref[0].astype(BF16), preferred_element_type=F32) + b_ref[0]


def _modulation(c, mod_w, mod_b):
    depth, d, n = mod_w.shape
    b = c.shape[0]
    tn = 1536
    return pl.pallas_call(
        _mod_kernel,
        grid=(depth, n // tn),
        in_specs=[pl.BlockSpec((b, d), lambda i, j: (0, 0)),
                  pl.BlockSpec((1, d, tn), lambda i, j: (i, 0, j)),
                  pl.BlockSpec((1, 1, tn), lambda i, j: (i, 0, j))],
        out_specs=pl.BlockSpec((1, b, tn), lambda i, j: (i, 0, j)),
        out_shape=jax.ShapeDtypeStruct((depth, b, n), F32),
        compiler_params=_params(("parallel", "parallel")),
        name="modulation",
    )(c, mod_w, mod_b.reshape(depth, 1, n))


def _rope_table_kernel(pos_ref, inv_ref, a_ref, b_ref, c_ref):
    ang = pos_ref[0].astype(F32) * inv_ref[...]
    cos, sin = jnp.cos(ang), jnp.sin(ang)
    lane = lax.broadcasted_iota(jnp.int32, ang.shape, 1)
    half = MLA_ROPE // 2
    zero = jnp.zeros_like(ang)
    a_ref[0] = jnp.where(lane < MLA_ROPE, cos, zero)
    b_ref[0] = jnp.where(lane < half, -sin, zero)
    c_ref[0] = jnp.where((lane >= half) & (lane < MLA_ROPE), sin, zero)


def _rope_tables(positions):
    b, s = positions.shape
    tm = 512
    inv_freq = 1.0 / (ROPE_THETA ** (jnp.arange(0, MLA_ROPE, 2, dtype=F32) / MLA_ROPE))
    inv = jnp.tile(inv_freq, LANES // (MLA_ROPE // 2)).reshape(1, LANES)
    spec = pl.BlockSpec((1, tm, LANES), lambda i, j: (i, j, 0))
    shp = jax.ShapeDtypeStruct((b, s, LANES), F32)
    return pl.pallas_call(
        _rope_table_kernel,
        grid=(b, s // tm),
        in_specs=[pl.BlockSpec((1, tm, 1), lambda i, j: (i, j, 0)),
                  pl.BlockSpec((1, LANES), lambda i, j: (0, 0))],
        out_specs=[spec, spec, spec],
        out_shape=[shp, shp, shp],
        compiler_params=_params(("parallel", "parallel")),
        name="rope_tables",
    )(positions.reshape(b, s, 1), inv)


def _rope(g, ta, tb, tc):
    return (g * ta + pltpu.roll(g, LANES - MLA_ROPE // 2, 1) * tb
            + pltpu.roll(g, MLA_ROPE // 2, 1) * tc)


def _mla_pre_kernel(x_ref, sc_ref, sh_ref, win_ref, qn_ref, wq_ref, kvn_ref, wkv_ref,
                    ta_ref, tb_ref, tc_ref, q_ref, k_ref, v_ref):
    h = _rms(x_ref[0]) * (1.0 + sc_ref[0]) + sh_ref[0]
    proj = jnp.dot(h.astype(BF16), win_ref[...], preferred_element_type=F32)
    cq = proj[:, :MLA_Q_LORA]
    ckv = proj[:, MLA_Q_LORA:MLA_Q_LORA + MLA_KV_LORA]
    kr = proj[:, MLA_Q_LORA + MLA_KV_LORA:]
    ta, tb, tc = ta_ref[0], tb_ref[0], tc_ref[0]
    qa = jnp.dot((_rms(cq) * qn_ref[...]).astype(BF16), wq_ref[...],
                 preferred_element_type=F32)
    kva = jnp.dot((_rms(ckv) * kvn_ref[...]).astype(BF16), wkv_ref[...],
                  preferred_element_type=F32)
    krp = _rope(kr, ta, tb, tc).astype(BF16)
    scale = (MLA_NOPE + MLA_ROPE) ** -0.5
    for hd in range(MLA_HEADS):
        o = hd * QK_PAD
        q_ref[0, hd, :, :MLA_NOPE] = (qa[:, o:o + MLA_NOPE] * scale).astype(BF16)
        q_ref[0, hd, :, MLA_NOPE:] = (_rope(qa[:, o + MLA_NOPE:o + QK_PAD], ta, tb, tc)
                                      * scale).astype(BF16)
        k_ref[0, hd, :, :MLA_NOPE] = kva[:, o:o + MLA_NOPE].astype(BF16)
        k_ref[0, hd, :, MLA_NOPE:] = krp
        v_ref[0, hd] = kva[:, o + MLA_NOPE:o + MLA_NOPE + MLA_V].astype(BF16)


def _mla_pre(x, sc, sh, w_in, q_norm, w_q, kv_norm, w_kv, tabs):
    b, s, d = x.shape
    tm = 512
    hq = MLA_HEADS
    row = lambda i, j: (i, j, 0)
    vec = pl.BlockSpec((1, 1, d), lambda i, j: (i, 0, 0))
    tab = pl.BlockSpec((1, tm, LANES), row)
    head_spec = lambda w: pl.BlockSpec((1, hq, tm, w), lambda i, j: (i, 0, j, 0))
    return pl.pallas_call(
        _mla_pre_kernel,
        grid=(b, s // tm),
        in_specs=[pl.BlockSpec((1, tm, d), row), vec, vec,
                  _const_spec(w_in.shape), _const_spec(q_norm.shape), _const_spec(w_q.shape),
                  _const_spec(kv_norm.shape), _const_spec(w_kv.shape), tab, tab, tab],
        out_specs=[head_spec(QK_PAD), head_spec(QK_PAD), head_spec(MLA_V)],
        out_shape=[jax.ShapeDtypeStruct((b, hq, s, QK_PAD), BF16),
                   jax.ShapeDtypeStruct((b, hq, s, QK_PAD), BF16),
                   jax.ShapeDtypeStruct((b, hq, s, MLA_V), BF16)],
        compiler_params=_params(("parallel", "parallel")),
        name="mla_pre",
    )(x, sc, sh, w_in, q_norm, w_q, kv_norm, w_kv, *tabs)


def _attn_kernel(q_ref, k_ref, v_ref, o_ref, *, tq):
    i = pl.program_id(2)
    q = q_ref[0, 0]

    def block(j, carry, masked):
        m, l, acc = carry
        start = pl.multiple_of(j * tq, tq)
        kb = k_ref[0, 0, pl.ds(start, tq), :]
        vb = v_ref[0, 0, pl.ds(start, tq), :]
        s = lax.dot_general(q, kb, (((1,), (1,)), ((), ())), preferred_element_type=F32)
        if masked:
            r = lax.broadcasted_iota(jnp.int32, s.shape, 0) // CHUNK
            c = lax.broadcasted_iota(jnp.int32, s.shape, 1) // CHUNK
            s = jnp.where(c <= r, s, -jnp.inf)
        m_new = jnp.maximum(m, jnp.max(s, axis=-1, keepdims=True))
        alpha = jnp.exp(m - m_new)
        p = jnp.exp(s - m_new)
        l = alpha * l + jnp.sum(p, axis=-1, keepdims=True)
        acc = alpha * acc + jnp.dot(p.astype(BF16), vb, preferred_element_type=F32)
        return m_new, l, acc

    init = (jnp.full((tq, 1), -1e30, F32), jnp.zeros((tq, 1), F32),
            jnp.zeros((tq, MLA_V), F32))
    carry = lax.fori_loop(0, i, lambda j, c: block(j, c, False), init)
    m, l, acc = block(i, carry, True)
    o_ref[0] = (acc / l).astype(o_ref.dtype)


def _attention(q, k, v):
    b, hq, s, _ = q.shape
    tq = 512
    return pl.pallas_call(
        functools.partial(_attn_kernel, tq=tq),
        grid=(b, hq, s // tq),
        in_specs=[pl.BlockSpec((1, 1, tq, QK_PAD), lambda bi, h, i: (bi, h, i, 0)),
                  pl.BlockSpec((1, 1, s, QK_PAD), lambda bi, h, i: (bi, h, 0, 0)),
                  pl.BlockSpec((1, 1, s, MLA_V), lambda bi, h, i: (bi, h, 0, 0))],
        out_specs=pl.BlockSpec((1, tq, MLA_V), lambda bi, h, i: (bi, i, h)),
        out_shape=jax.ShapeDtypeStruct((b, s, hq * MLA_V), BF16),
        compiler_params=_params(("parallel", "parallel", "parallel")),
        name="mla_attention",
    )(q, k, v)


def _log_sigmoid(x):
    return jnp.minimum(x, 0.0) - jnp.log1p(jnp.exp(-jnp.abs(x)))


def _ml_pre_kernel(x_ref, sc_ref, sh_ref, w_ref, wg_ref, wgt_ref, bg_ref, bgt_ref,
                   q_ref, k_ref, v_ref, og_ref, gc_ref, gr_ref, *, dk_all, dv_all):
    h = (_rms(x_ref[0]) * (1.0 + sc_ref[0]) + sh_ref[0]).astype(BF16)
    proj = jnp.dot(h, w_ref[...], preferred_element_type=F32)
    dk = dk_all // ML_HEADS
    q_ref[0] = proj[:, :dk_all].astype(BF16)
    k_ref[0] = (proj[:, dk_all:2 * dk_all] * dk ** -0.5).astype(BF16)
    v_ref[0] = proj[:, 2 * dk_all:2 * dk_all + dv_all].astype(BF16)
    og_ref[0] = proj[:, 2 * dk_all + dv_all:]
    gcol = jnp.dot(h, wg_ref[...], preferred_element_type=F32) + bg_ref[...]
    lane = lax.broadcasted_iota(jnp.int32, gcol.shape, 1)
    gc_ref[0] = jnp.where(lane < ML_HEADS, gcol, _log_sigmoid(gcol))
    grow = lax.dot_general(wgt_ref[...], h, (((1,), (1,)), ((), ())),
                           preferred_element_type=F32) + bgt_ref[...]
    sub = lax.broadcasted_iota(jnp.int32, grow.shape, 0)
    gr_ref[0] = jnp.where(sub < ML_HEADS, grow, _log_sigmoid(grow))


def _ml_pre(x, sc, sh, w_main, w_g, w_gt, b_g, b_gt, dk_all, dv_all):
    b, s, d = x.shape
    tm = 512
    row = lambda i, j: (i, j, 0)
    vec = pl.BlockSpec((1, 1, d), lambda i, j: (i, 0, 0))
    ng = 2 * ML_HEADS
    return pl.pallas_call(
        functools.partial(_ml_pre_kernel, dk_all=dk_all, dv_all=dv_all),
        grid=(b, s // tm),
        in_specs=[pl.BlockSpec((1, tm, d), row), vec, vec,
                  _const_spec(w_main.shape), _const_spec(w_g.shape), _const_spec(w_gt.shape),
                  _const_spec(b_g.shape), _const_spec(b_gt.shape)],
        out_specs=[pl.BlockSpec((1, tm, dk_all), row), pl.BlockSpec((1, tm, dk_all), row),
                   pl.BlockSpec((1, tm, dv_all), row), pl.BlockSpec((1, tm, dv_all), row),
                   pl.BlockSpec((1, tm, LANES), row),
                   pl.BlockSpec((1, ng, tm), lambda i, j: (i, 0, j))],
        out_shape=[jax.ShapeDtypeStruct((b, s, dk_all), BF16),
                   jax.ShapeDtypeStruct((b, s, dk_all), BF16),
                   jax.ShapeDtypeStruct((b, s, dv_all), BF16),
                   jax.ShapeDtypeStruct((b, s, dv_all), F32),
                   jax.ShapeDtypeStruct((b, s, LANES), F32),
                   jax.ShapeDtypeStruct((b, ng, s), F32)],
        compiler_params=_params(("parallel", "parallel")),
        name="mlstm_pre",
    )(x, sc, sh, w_main, w_g, w_gt, b_g, b_gt)


def _ml_scan_kernel(q_ref, k_ref, v_ref, og_ref, gc_ref, gr_ref, hn_ref, y_ref,
                    c_scr, n_scr, m_scr, *, dk, dv):
    L = q_ref.shape[1]

    @pl.when(pl.program_id(1) == 0)
    def _():
        c_scr[...] = jnp.zeros_like(c_scr)
        n_scr[...] = jnp.zeros_like(n_scr)
        m_scr[...] = jnp.zeros_like(m_scr)

    row = lax.broadcasted_iota(jnp.int32, (L, L), 0)
    col = lax.broadcasted_iota(jnp.int32, (L, L), 1)
    causal = col <= row
    tri = causal.astype(F32)
    tri_t = (row <= col).astype(F32)
    gc = gc_ref[0]
    gr = gr_ref[0]
    bcol_all = jnp.dot(tri, gc, precision=lax.Precision.HIGHEST, preferred_element_type=F32)
    brow_all = jnp.dot(gr, tri_t, precision=lax.Precision.HIGHEST, preferred_element_type=F32)

    for h in range(ML_HEADS):
        b_col = bcol_all[:, ML_HEADS + h:ML_HEADS + h + 1]
        i_col = gc[:, h:h + 1]
        b_row = brow_all[ML_HEADS + h:ML_HEADS + h + 1, :]
        i_row = gr[h:h + 1, :]
        m_prev = m_scr[h:h + 1, 0:1]
        n_row = n_scr[h:h + 1, :]
        qh = q_ref[0, :, h * dk:(h + 1) * dk]
        kh = k_ref[0, :, h * dk:(h + 1) * dk]
        vh = v_ref[0, :, h * dv:(h + 1) * dv]

        dmat = jnp.where(causal, b_col - b_row + i_row, -jnp.inf)
        inter_log = b_col + m_prev
        m_t = jnp.maximum(inter_log, jnp.max(dmat, axis=-1, keepdims=True))
        inter_w = jnp.exp(inter_log - m_t)
        s_mat = lax.dot_general(qh, kh, (((1,), (1,)), ((), ())),
                                preferred_element_type=F32) * jnp.exp(dmat - m_t)
        c_state = c_scr[h]
        num = (inter_w * jnp.dot(qh, c_state.astype(BF16), preferred_element_type=F32)
               + jnp.dot(s_mat.astype(BF16), vh, preferred_element_type=F32))
        den = (inter_w * jnp.sum(qh.astype(F32) * n_row, axis=-1, keepdims=True)
               + jnp.sum(s_mat, axis=-1, keepdims=True))
        h_out = num / jnp.maximum(jnp.abs(den), jnp.exp(-m_t))

        hs = _rms(h_out) * hn_ref[:, h * dv:(h + 1) * dv]
        y = hs * jax.nn.sigmoid(og_ref[0, :, h * dv:(h + 1) * dv])
        y_ref[0, :, h * dv:(h + 1) * dv] = y.astype(y_ref.dtype)

        b_last = b_col[L - 1:L, :]
        w_log = b_last - b_row + i_row
        m_new = jnp.maximum(b_last + m_prev, jnp.max(w_log, axis=-1, keepdims=True))
        decay = jnp.exp(b_last + m_prev - m_new)
        ws_col = jnp.exp(b_last - b_col + i_col - m_new)
        ws_row = jnp.exp(w_log - m_new)
        wv = (ws_col * vh.astype(F32)).astype(BF16)
        c_scr[h] = decay * c_state + lax.dot_general(
            kh, wv, (((0,), (0,)), ((), ())), preferred_element_type=F32)
        ws8 = jnp.broadcast_to(ws_row, (SUBLANES, L)).astype(BF16)
        n_upd = jnp.dot(ws8, kh, preferred_element_type=F32)[0:1, :]
        n_scr[h:h + 1, :] = decay * n_row + n_upd
        m_scr[h:h + 1, :] = jnp.broadcast_to(m_new, (1, m_scr.shape[1]))


def _ml_scan(q, k, v, og, gc, gr, head_norm):
    b, s, dk_all = q.shape
    dv_all = v.shape[-1]
    dk, dv = dk_all // ML_HEADS, dv_all // ML_HEADS
    L = ML_CHUNK
    row = lambda i, j: (i, j, 0)
    return pl.pallas_call(
        functools.partial(_ml_scan_kernel, dk=dk, dv=dv),
        grid=(b, s // L),
        in_specs=[pl.BlockSpec((1, L, dk_all), row), pl.BlockSpec((1, L, dk_all), row),
                  pl.BlockSpec((1, L, dv_all), row), pl.BlockSpec((1, L, dv_all), row),
                  pl.BlockSpec((1, L, LANES), row),
                  pl.BlockSpec((1, 2 * ML_HEADS, L), lambda i, j: (i, 0, j)),
                  _const_spec(head_norm.shape)],
        out_specs=pl.BlockSpec((1, L, dv_all), row),
        out_shape=jax.ShapeDtypeStruct((b, s, dv_all), BF16),
        scratch_shapes=[pltpu.VMEM((ML_HEADS, dk, dv), F32),
                        pltpu.VMEM((SUBLANES, dk), F32),
                        pltpu.VMEM((SUBLANES, LANES), F32)],
        compiler_params=_params(("parallel", "arbitrary")),
        name="mlstm_scan",
    )(q, k, v, og, gc, gr, head_norm)


def _post_ffn_kernel(x_ref, u_ref, wo_ref, ga_ref, sc_ref, sh_ref, gf_ref, wup_ref, cw_ref,
                     cb_ref, wdn_ref, fn_ref, o_ref, a_scr, p_scr, *, ck, final):
    tm = x_ref.shape[1]
    dff = p_scr.shape[1]
    halo = SUBLANES

    @pl.when(pl.program_id(1) == 0)
    def _():
        a_scr[0:halo, :] = jnp.zeros((halo, dff), F32)

    y = jnp.dot(u_ref[0], wo_ref[...], preferred_element_type=F32)
    x1 = x_ref[0] + ga_ref[0] * y
    h = (_rms(x1) * (1.0 + sc_ref[0]) + sh_ref[0]).astype(BF16)
    for c in range(dff // ck):
        lo = c * ck
        a = jnp.dot(h, wup_ref[:, lo:lo + ck], preferred_element_type=F32)
        g = jnp.dot(h, wup_ref[:, dff + lo:dff + lo + ck], preferred_element_type=F32)
        a_scr[halo:halo + tm, lo:lo + ck] = a
        a1 = a_scr[halo - 1:halo - 1 + tm, lo:lo + ck]
        a2 = a_scr[halo - 2:halo - 2 + tm, lo:lo + ck]
        conv = (cw_ref[0:1, lo:lo + ck] * a2 + cw_ref[1:2, lo:lo + ck] * a1
                + cw_ref[2:3, lo:lo + ck] * a + cb_ref[:, lo:lo + ck])
        gelu = 0.5 * conv * (1.0 + lax.erf(conv * (2.0 ** -0.5)))
        p_scr[:, lo:lo + ck] = (gelu * g).astype(BF16)
        a_scr[0:halo, lo:lo + ck] = a_scr[tm:tm + halo, lo:lo + ck]
    f = jnp.dot(p_scr[...], wdn_ref[...], preferred_element_type=F32)
    x2 = x1 + gf_ref[0] * f
    if final:
        x2 = _rms(x2) * fn_ref[...]
    o_ref[0] = x2


def _post_ffn(x, u, w_out, g_a, sc_f, sh_f, g_f, w_up, conv_w, conv_b, w_down, final_norm,
              final):
    b, s, d = x.shape
    dff = w_down.shape[0]
    tm = 512
    ck = 256
    row = lambda i, j: (i, j, 0)
    vec = pl.BlockSpec((1, 1, d), lambda i, j: (i, 0, 0))
    return pl.pallas_call(
        functools.partial(_post_ffn_kernel, ck=ck, final=final),
        grid=(b, s // tm),
        in_specs=[pl.BlockSpec((1, tm, d), row), pl.BlockSpec((1, tm, d), row),
                  _const_spec(w_out.shape), vec, vec, vec, vec, _const_spec(w_up.shape),
                  _const_spec(conv_w.shape), _const_spec(conv_b.shape),
                  _const_spec(w_down.shape), _const_spec(final_norm.shape)],
        out_specs=pl.BlockSpec((1, tm, d), row),
        out_shape=jax.ShapeDtypeStruct((b, s, d), F32),
        scratch_shapes=[pltpu.VMEM((tm + SUBLANES, dff), F32), pltpu.VMEM((tm, dff), BF16)],
        compiler_params=_params(("parallel", "arbitrary")),
        name="post_ffn",
    )(x, u, w_out, g_a, sc_f, sh_f, g_f, w_up, conv_w, conv_b, w_down, final_norm)


def kernel(x, c, positions, mod_w, mod_b, mla_w_in, mla_q_norm, mla_w_q_up, mla_kv_norm,
           mla_w_kv_up, mla_w_out, ml_w_in, ml_b_gates, ml_head_norm, ml_w_out, ffn_w_up,
           ffn_conv_w, ffn_conv_b, ffn_w_down, final_norm):
    b, s, d = x.shape
    depth = mod_w.shape[0]
    dff = ffn_w_down.shape[1]
    dv_all = ml_w_out.shape[1]
    dk_all = (ml_w_in.shape[2] - 2 * dv_all - 2 * ML_HEADS) // 2

    mod = _modulation(c, mod_w, mod_b)
    tabs = _rope_tables(positions)
    fn = final_norm.reshape(1, d)

    for i in range(depth):
        sh_a, sc_a, g_a, sh_f, sc_f, g_f = [
            mod[i, :, t * d:(t + 1) * d].reshape(b, 1, d) for t in range(6)]
        j = i // 2
        if i % 2 == 0:
            n_in = mla_w_in.shape[2]
            w_in = jnp.pad(mla_w_in[j], ((0, 0), (0, -n_in % LANES))).astype(BF16)
            hd = MLA_NOPE + MLA_ROPE
            w_q = jnp.pad(mla_w_q_up[j].reshape(MLA_Q_LORA, MLA_HEADS, hd),
                          ((0, 0), (0, 0), (0, QK_PAD - hd)))
            w_q = w_q.reshape(MLA_Q_LORA, MLA_HEADS * QK_PAD).astype(BF16)
            q, k, v = _mla_pre(x, sc_a, sh_a, w_in, mla_q_norm[j].reshape(1, -1), w_q,
                               mla_kv_norm[j].reshape(1, -1), mla_w_kv_up[j].astype(BF16), tabs)
            u = _attention(q, k, v)
            w_out = mla_w_out[j].astype(BF16)
        else:
            ng = 2 * ML_HEADS
            n_main = 2 * dk_all + 2 * dv_all
            w_main = ml_w_in[j][:, :n_main].astype(BF16)
            w_gt = ml_w_in[j][:, n_main:].T.astype(BF16)
            w_g = jnp.pad(w_gt.T, ((0, 0), (0, LANES - ng)))
            b_g = jnp.pad(ml_b_gates[j], (0, LANES - ng)).reshape(1, LANES)
            b_gt = ml_b_gates[j].reshape(ng, 1)
            q, k, v, og, gc, gr = _ml_pre(x, sc_a, sh_a, w_main, w_g, w_gt, b_g, b_gt,
                                          dk_all, dv_all)
            u = _ml_scan(q, k, v, og, gc, gr, ml_head_norm[j].reshape(1, dv_all))
            w_out = ml_w_out[j].astype(BF16)
        x = _post_ffn(x, u, w_out, g_a, sc_f, sh_f, g_f, ffn_w_up[i].astype(BF16),
                      ffn_conv_w[i], ffn_conv_b[i].reshape(1, dff),
                      ffn_w_down[i].astype(BF16), fn, final=(i == depth - 1))
    return x
```

```python
import functools
import math

import jax
import jax.numpy as jnp
from jax import lax
from jax.experimental import pallas as pl
from jax.experimental.pallas import tpu as pltpu

EPS = 1e-6
ROPE_THETA = 10000.0
CHUNK = 64
MLA_HEADS = 8
MLA_Q_LORA = 512
MLA_KV_LORA = 256
MLA_NOPE = 128
MLA_ROPE = 64
MLA_V = 128
ML_HEADS = 4
CONV_W = 3

LANES = 128
SUBLANES = 8
QK_PAD = 256
ML_CHUNK = 256
VMEM_LIMIT = 56 * 1024 * 1024

BF16 = jnp.bfloat16
F32 = jnp.float32


def _params(sem):
    return pltpu.CompilerParams(dimension_semantics=sem, vmem_limit_bytes=VMEM_LIMIT)


def _const_spec(shape):
    nd = len(shape)
    return pl.BlockSpec(shape, lambda *_: (0,) * nd, pipeline_mode=pl.Buffered(1))


def _rms(x):
    return x * lax.rsqrt(jnp.mean(x * x, axis=-1, keepdims=True) + EPS)


def _mod_kernel(c_ref, w_ref, b_ref, o_ref):
    c = c_ref[...]
    ca = (c * jax.nn.sigmoid(c)).astype(BF16)
    o_ref[0] = jnp.dot(ca, w_ref[0].astype(BF16), preferred_element_type=F32) + b_ref[0]


def _modulation(c, mod_w, mod_b):
    depth, d, n = mod_w.shape
    b = c.shape[0]
    tn = 1536
    return pl.pallas_call(
        _mod_kernel,
        grid=(depth, n // tn),
        in_specs=[pl.BlockSpec((b, d), lambda i, j: (0, 0)),
                  pl.BlockSpec((1, d, tn), lambda i, j: (i, 0, j)),
                  pl.BlockSpec((1, 1, tn), lambda i, j: (i, 0, j))],
        out_specs=pl.BlockSpec((1, b, tn), lambda i, j: (i, 0, j)),
        out_shape=jax.ShapeDtypeStruct((depth, b, n), F32),
        compiler_params=_params(("parallel", "parallel")),
        name="modulation",
    )(c, mod_w, mod_b.reshape(depth, 1, n))


def _rope_table_kernel(pos_ref, inv_ref, a_ref, b_ref, c_ref):
    ang = pos_ref[0].astype(F32) * inv_ref[...]
    cos, sin = jnp.cos(ang), jnp.sin(ang)
    lane = lax.broadcasted_iota(jnp.int32, ang.shape, 1)
    half = MLA_ROPE // 2
    zero = jnp.zeros_like(ang)
    a_ref[0] = jnp.where(lane < MLA_ROPE, cos, zero)
    b_ref[0] = jnp.where(lane < half, -sin, zero)
    c_ref[0] = jnp.where((lane >= half) & (lane < MLA_ROPE), sin, zero)


def _rope_tables(positions):
    b, s = positions.shape
    tm = 512
    inv_freq = 1.0 / (ROPE_THETA ** (jnp.arange(0, MLA_ROPE, 2, dtype=F32) / MLA_ROPE))
    inv = jnp.tile(inv_freq, LANES // (MLA_ROPE // 2)).reshape(1, LANES)
    spec = pl.BlockSpec((1, tm, LANES), lambda i, j: (i, j, 0))
    shp = jax.ShapeDtypeStruct((b, s, LANES), F32)
    return pl.pallas_call(
        _rope_table_kernel,
        grid=(b, s // tm),
        in_specs=[pl.BlockSpec((1, tm, 1), lambda i, j: (i, j, 0)),
                  pl.BlockSpec((1, LANES), lambda i, j: (0, 0))],
        out_specs=[spec, spec, spec],
        out_shape=[shp, shp, shp],
        compiler_params=_params(("parallel", "parallel")),
        name="rope_tables",
    )(positions.reshape(b, s, 1), inv)


def _rope(g, ta, tb, tc):
    return (g * ta + pltpu.roll(g, LANES - MLA_ROPE // 2, 1) * tb
            + pltpu.roll(g, MLA_ROPE // 2, 1) * tc)


def _mla_pre_kernel(x_ref, sc_ref, sh_ref, win_ref, qn_ref, wq_ref, kvn_ref, wkv_ref,
                    ta_ref, tb_ref, tc_ref, q_ref, k_ref, v_ref):
    h = _rms(x_ref[0]) * (1.0 + sc_ref[0]) + sh_ref[0]
    proj = jnp.dot(h.astype(BF16), win_ref[...], preferred_element_type=F32)
    cq = proj[:, :MLA_Q_LORA]
    ckv = proj[:, MLA_Q_LORA:MLA_Q_LORA + MLA_KV_LORA]
    kr = proj[:, MLA_Q_LORA + MLA_KV_LORA:]
    ta, tb, tc = ta_ref[0], tb_ref[0], tc_ref[0]
    qa = jnp.dot((_rms(cq) * qn_ref[...]).astype(BF16), wq_ref[...],
                 preferred_element_type=F32)
    kva = jnp.dot((_rms(ckv) * kvn_ref[...]).astype(BF16), wkv_ref[...],
                  preferred_element_type=F32)
    krp = _rope(kr, ta, tb, tc).astype(BF16)
    scale = (MLA_NOPE + MLA_ROPE) ** -0.5
    for hd in range(MLA_HEADS):
        o = hd * QK_PAD
        q_ref[0, hd, :, :MLA_NOPE] = (qa[:, o:o + MLA_NOPE] * scale).astype(BF16)
        q_ref[0, hd, :, MLA_NOPE:] = (_rope(qa[:, o + MLA_NOPE:o + QK_PAD], ta, tb, tc)
                                      * scale).astype(BF16)
        k_ref[0, hd, :, :MLA_NOPE] = kva[:, o:o + MLA_NOPE].astype(BF16)
        k_ref[0, hd, :, MLA_NOPE:] = krp
        v_ref[0, hd] = kva[:, o + MLA_NOPE:o + MLA_NOPE + MLA_V].astype(BF16)


def _mla_pre(x, sc, sh, w_in, q_norm, w_q, kv_norm, w_kv, tabs):
    b, s, d = x.shape
    tm = 512
    hq = MLA_HEADS
    row = lambda i, j: (i, j, 0)
    vec = pl.BlockSpec((1, 1, d), lambda i, j: (i, 0, 0))
    tab = pl.BlockSpec((1, tm, LANES), row)
    head_spec = lambda w: pl.BlockSpec((1, hq, tm, w), lambda i, j: (i, 0, j, 0))
    return pl.pallas_call(
        _mla_pre_kernel,
        grid=(b, s // tm),
        in_specs=[pl.BlockSpec((1, tm, d), row), vec, vec,
                  _const_spec(w_in.shape), _const_spec(q_norm.shape), _const_spec(w_q.shape),
                  _const_spec(kv_norm.shape), _const_spec(w_kv.shape), tab, tab, tab],
        out_specs=[head_spec(QK_PAD), head_spec(QK_PAD), head_spec(MLA_V)],
        out_shape=[jax.ShapeDtypeStruct((b, hq, s, QK_PAD), BF16),
                   jax.ShapeDtypeStruct((b, hq, s, QK_PAD), BF16),
                   jax.ShapeDtypeStruct((b, hq, s, MLA_V), BF16)],
        compiler_params=_params(("parallel", "parallel")),
        name="mla_pre",
    )(x, sc, sh, w_in, q_norm, w_q, kv_norm, w_kv, *tabs)


def _attn_kernel(q_ref, k_ref, v_ref, o_ref, *, tq):
    s_len = q_ref.shape[2]
    for i in range(s_len // tq):
        q = q_ref[0, 0, i * tq:(i + 1) * tq, :]
        m = jnp.full((tq, 1), -1e30, F32)
        l = jnp.zeros((tq, 1), F32)
        acc = jnp.zeros((tq, MLA_V), F32)
        for j in range(i + 1):
            kb = k_ref[0, 0, j * tq:(j + 1) * tq, :]
            vb = v_ref[0, 0, j * tq:(j + 1) * tq, :]
            s = lax.dot_general(q, kb, (((1,), (1,)), ((), ())), preferred_element_type=F32)
            if j == i:
                r = lax.broadcasted_iota(jnp.int32, s.shape, 0) // CHUNK
                c = lax.broadcasted_iota(jnp.int32, s.shape, 1) // CHUNK
                s = jnp.where(c <= r, s, -jnp.inf)
            m_new = jnp.maximum(m, jnp.max(s, axis=-1, keepdims=True))
            alpha = jnp.exp(m - m_new)
            p = jnp.exp(s - m_new)
            l = alpha * l + jnp.sum(p, axis=-1, keepdims=True)
            acc = alpha * acc + jnp.dot(p.astype(BF16), vb, preferred_element_type=F32)
            m = m_new
        o_ref[0, i * tq:(i + 1) * tq, :] = (acc / l).astype(o_ref.dtype)


def _attention(q, k, v):
    b, hq, s, _ = q.shape
    tq = 512
    return pl.pallas_call(
        functools.partial(_attn_kernel, tq=tq),
        grid=(b, hq),
        in_specs=[pl.BlockSpec((1, 1, s, QK_PAD), lambda bi, h: (bi, h, 0, 0)),
                  pl.BlockSpec((1, 1, s, QK_PAD), lambda bi, h: (bi, h, 0, 0)),
                  pl.BlockSpec((1, 1, s, MLA_V), lambda bi, h: (bi, h, 0, 0))],
        out_specs=pl.BlockSpec((1, s, MLA_V), lambda bi, h: (bi, 0, h)),
        out_shape=jax.ShapeDtypeStruct((b, s, hq * MLA_V), BF16),
        compiler_params=_params(("parallel", "parallel")),
        name="mla_attention",
    )(q, k, v)


def _log_sigmoid(x):
    return jnp.minimum(x, 0.0) - jnp.log1p(jnp.exp(-jnp.abs(x)))


def _ml_pre_kernel(x_ref, sc_ref, sh_ref, w_ref, wg_ref, wgt_ref, bg_ref, bgt_ref,
                   q_ref, k_ref, v_ref, og_ref, gc_ref, gr_ref, *, dk_all, dv_all):
    h = (_rms(x_ref[0]) * (1.0 + sc_ref[0]) + sh_ref[0]).astype(BF16)
    proj = jnp.dot(h, w_ref[...], preferred_element_type=F32)
    dk = dk_all // ML_HEADS
    q_ref[0] = proj[:, :dk_all].astype(BF16)
    k_ref[0] = (proj[:, dk_all:2 * dk_all] * dk ** -0.5).astype(BF16)
    v_ref[0] = proj[:, 2 * dk_all:2 * dk_all + dv_all].astype(BF16)
    og_ref[0] = proj[:, 2 * dk_all + dv_all:]
    gcol = jnp.dot(h, wg_ref[...], preferred_element_type=F32) + bg_ref[...]
    lane = lax.broadcasted_iota(jnp.int32, gcol.shape, 1)
    gc_ref[0] = jnp.where(lane < ML_HEADS, gcol, _log_sigmoid(gcol))
    grow = lax.dot_general(wgt_ref[...], h, (((1,), (1,)), ((), ())),
                           preferred_element_type=F32) + bgt_ref[...]
    sub = lax.broadcasted_iota(jnp.int32, grow.shape, 0)
    gr_ref[0] = jnp.where(sub < ML_HEADS, grow, _log_sigmoid(grow))


def _ml_pre(x, sc, sh, w_main, w_g, w_gt, b_g, b_gt, dk_all, dv_all):
    b, s, d = x.shape
    tm = 512
    row = lambda i, j: (i, j, 0)
    vec = pl.BlockSpec((1, 1, d), lambda i, j: (i, 0, 0))
    ng = 2 * ML_HEADS
    return pl.pallas_call(
        functools.partial(_ml_pre_kernel, dk_all=dk_all, dv_all=dv_all),
        grid=(b, s // tm),
        in_specs=[pl.BlockSpec((1, tm, d), row), vec, vec,
                  _const_spec(w_main.shape), _const_spec(w_g.shape), _const_spec(w_gt.shape),
                  _const_spec(b_g.shape), _const_spec(b_gt.shape)],
        out_specs=[pl.BlockSpec((1, tm, dk_all), row), pl.BlockSpec((1, tm, dk_all), row),
                   pl.BlockSpec((1, tm, dv_all), row), pl.BlockSpec((1, tm, dv_all), row),
                   pl.BlockSpec((1, tm, LANES), row),
                   pl.BlockSpec((1, ng, tm), lambda i, j: (i, 0, j))],
        out_shape=[jax.ShapeDtypeStruct((b, s, dk_all), BF16),
                   jax.ShapeDtypeStruct((b, s, dk_all), BF16),
                   jax.ShapeDtypeStruct((b, s, dv_all), BF16),
                   jax.ShapeDtypeStruct((b, s, dv_all), F32),
                   jax.ShapeDtypeStruct((b, s, LANES), F32),
                   jax.ShapeDtypeStruct((b, ng, s), F32)],
        compiler_params=_params(("parallel", "parallel")),
        name="mlstm_pre",
    )(x, sc, sh, w_main, w_g, w_gt, b_g, b_gt)


def _ml_scan_kernel(q_ref, k_ref, v_ref, og_ref, gc_ref, gr_ref, hn_ref, y_ref,
                    c_scr, n_scr, m_scr, *, dk, dv):
    L = q_ref.shape[1]

    @pl.when(pl.program_id(1) == 0)
    def _():
        c_scr[...] = jnp.zeros_like(c_scr)
        n_scr[...] = jnp.zeros_like(n_scr)
        m_scr[...] = jnp.zeros_like(m_scr)

    row = lax.broadcasted_iota(jnp.int32, (L, L), 0)
    col = lax.broadcasted_iota(jnp.int32, (L, L), 1)
    causal = col <= row
    tri = causal.astype(F32)
    tri_t = (row <= col).astype(F32)
    gc = gc_ref[0]
    gr = gr_ref[0]
    bcol_all = jnp.dot(tri, gc, precision=lax.Precision.HIGHEST, preferred_element_type=F32)
    brow_all = jnp.dot(gr, tri_t, precision=lax.Precision.HIGHEST, preferred_element_type=F32)

    for h in range(ML_HEADS):
        b_col = bcol_all[:, ML_HEADS + h:ML_HEADS + h + 1]
        i_col = gc[:, h:h + 1]
        b_row = brow_all[ML_HEADS + h:ML_HEADS + h + 1, :]
        i_row = gr[h:h + 1, :]
        m_prev = m_scr[h:h + 1, 0:1]
        n_row = n_scr[h:h + 1, :]
        qh = q_ref[0, :, h * dk:(h + 1) * dk]
        kh = k_ref[0, :, h * dk:(h + 1) * dk]
        vh = v_ref[0, :, h * dv:(h + 1) * dv]

        dmat = jnp.where(causal, b_col - b_row + i_row, -jnp.inf)
        inter_log = b_col + m_prev
        m_t = jnp.maximum(inter_log, jnp.max(dmat, axis=-1, keepdims=True))
        inter_w = jnp.exp(inter_log - m_t)
        s_mat = lax.dot_general(qh, kh, (((1,), (1,)), ((), ())),
                                preferred_element_type=F32) * jnp.exp(dmat - m_t)
        c_state = c_scr[h]
        num = (inter_w * jnp.dot(qh, c_state.astype(BF16), preferred_element_type=F32)
               + jnp.dot(s_mat.astype(BF16), vh, preferred_element_type=F32))
        den = (inter_w * jnp.sum(qh.astype(F32) * n_row, axis=-1, keepdims=True)
               + jnp.sum(s_mat, axis=-1, keepdims=True))
        h_out = num / jnp.maximum(jnp.abs(den), jnp.exp(-m_t))

        hs = _rms(h_out) * hn_ref[:, h * dv:(h + 1) * dv]
        y = hs * jax.nn.sigmoid(og_ref[0, :, h * dv:(h + 1) * dv])
        y_ref[0, :, h * dv:(h + 1) * dv] = y.astype(y_ref.dtype)

        b_last = b_col[L - 1:L, :]
        w_log = b_last - b_row + i_row
        m_new = jnp.maximum(b_last + m_prev, jnp.max(w_log, axis=-1, keepdims=True))
        decay = jnp.exp(b_last + m_prev - m_new)
        ws_col = jnp.exp(b_last - b_col + i_col - m_new)
        ws_row = jnp.exp(w_log - m_new)
        wv = (ws_col * vh.astype(F32)).astype(BF16)
        c_scr[h] = decay * c_state + lax.dot_general(
            kh, wv, (((0,), (0,)), ((), ())), preferred_element_type=F32)
        ws8 = jnp.broadcast_to(ws_row, (SUBLANES, L)).astype(BF16)
        n_upd = jnp.dot(ws8, kh, preferred_element_type=F32)[0:1, :]
        n_scr[h:h + 1, :] = decay * n_row + n_upd
        m_scr[h:h + 1, :] = jnp.broadcast_to(m_new, (1, m_scr.shape[1]))


def _ml_scan(q, k, v, og, gc, gr, head_norm):
    b, s, dk_all = q.shape
    dv_all = v.shape[-1]
    dk, dv = dk_all // ML_HEADS, dv_all // ML_HEADS
    L = ML_CHUNK
    row = lambda i, j: (i, j, 0)
    return pl.pallas_call(
        functools.partial(_ml_scan_kernel, dk=dk, dv=dv),
        grid=(b, s // L),
        in_specs=[pl.BlockSpec((1, L, dk_all), row), pl.BlockSpec((1, L, dk_all), row),
                  pl.BlockSpec((1, L, dv_all), row), pl.BlockSpec((1, L, dv_all), row),
                  pl.BlockSpec((1, L, LANES), row),
                  pl.BlockSpec((1, 2 * ML_HEADS, L), lambda i, j: (i, 0, j)),
                  _const_spec(head_norm.shape)],
        out_specs=pl.BlockSpec((1, L, dv_all), row),
        out_shape=jax.ShapeDtypeStruct((b, s, dv_all), BF16),
        scratch_shapes=[pltpu.VMEM((ML_HEADS, dk, dv), F32),
                        pltpu.VMEM((SUBLANES, dk), F32),
                        pltpu.VMEM((SUBLANES, LANES), F32)],
        compiler_params=_params(("parallel", "arbitrary")),
        name="mlstm_scan",
    )(q, k, v, og, gc, gr, head_norm)


def _post_ffn_kernel(x_ref, u_ref, wo_ref, ga_ref, sc_ref, sh_ref, gf_ref, wup_ref, cw_ref,
                     cb_ref, wdn_ref, fn_ref, o_ref, a_scr, p_scr, *, ck, final):
    tm = x_ref.shape[1]
    dff = p_scr.shape[1]
    halo = SUBLANES

    @pl.when(pl.program_id(1) == 0)
    def _():
        a_scr[0:halo, :] = jnp.zeros((halo, dff), F32)

    y = jnp.dot(u_ref[0], wo_ref[...], preferred_element_type=F32)
    x1 = x_ref[0] + ga_ref[0] * y
    h = (_rms(x1) * (1.0 + sc_ref[0]) + sh_ref[0]).astype(BF16)
    for c in range(dff // ck):
        lo = c * ck
        a = jnp.dot(h, wup_ref[:, lo:lo + ck], preferred_element_type=F32)
        g = jnp.dot(h, wup_ref[:, dff + lo:dff + lo + ck], preferred_element_type=F32)
        a_scr[halo:halo + tm, lo:lo + ck] = a
        a1 = a_scr[halo - 1:halo - 1 + tm, lo:lo + ck]
        a2 = a_scr[halo - 2:halo - 2 + tm, lo:lo + ck]
        conv = (cw_ref[0:1, lo:lo + ck] * a2 + cw_ref[1:2, lo:lo + ck] * a1
                + cw_ref[2:3, lo:lo + ck] * a + cb_ref[:, lo:lo + ck])
        gelu = 0.5 * conv * (1.0 + lax.erf(conv * (2.0 ** -0.5)))
        p_scr[:, lo:lo + ck] = (gelu * g).astype(BF16)
        a_scr[0:halo, lo:lo + ck] = a_scr[tm:tm + halo, lo:lo + ck]
    f = jnp.dot(p_scr[...], wdn_ref[...], preferred_element_type=F32)
    x2 = x1 + gf_ref[0] * f
    if final:
        x2 = _rms(x2) * fn_ref[...]
    o_ref[0] = x2


def _post_ffn(x, u, w_out, g_a, sc_f, sh_f, g_f, w_up, conv_w, conv_b, w_down, final_norm,
              final):
    b, s, d = x.shape
    dff = w_down.shape[0]
    tm = 512
    ck = 256
    row = lambda i, j: (i, j, 0)
    vec = pl.BlockSpec((1, 1, d), lambda i, j: (i, 0, 0))
    return pl.pallas_call(
        functools.partial(_post_ffn_kernel, ck=ck, final=final),
        grid=(b, s // tm),
        in_specs=[pl.BlockSpec((1, tm, d), row), pl.BlockSpec((1, tm, d), row),
                  _const_spec(w_out.shape), vec, vec, vec, vec, _const_spec(w_up.shape),
                  _const_spec(conv_w.shape), _const_spec(conv_b.shape),
                  _const_spec(w_down.shape), _const_spec(final_norm.shape)],
        out_specs=pl.BlockSpec((1, tm, d), row),
        out_shape=jax.ShapeDtypeStruct((b, s, d), F32),
        scratch_shapes=[pltpu.VMEM((tm + SUBLANES, dff), F32), pltpu.VMEM((tm, dff), BF16)],
        compiler_params=_params(("parallel", "arbitrary")),
        name="post_ffn",
    )(x, u, w_out, g_a, sc_f, sh_f, g_f, w_up, conv_w, conv_b, w_down, final_norm)


def kernel(x, c, positions, mod_w, mod_b, mla_w_in, mla_q_norm, mla_w_q_up, mla_kv_norm,
           mla_w_kv_up, mla_w_out, ml_w_in, ml_b_gates, ml_head_norm, ml_w_out, ffn_w_up,
           ffn_conv_w, ffn_conv_b, ffn_w_down, final_norm):
    b, s, d = x.shape
    depth = mod_w.shape[0]
    dff = ffn_w_down.shape[1]
    dv_all = ml_w_out.shape[1]
    dk_all = (ml_w_in.shape[2] - 2 * dv_all - 2 * ML_HEADS) // 2

    mod = _modulation(c, mod_w, mod_b)
    tabs = _rope_tables(positions)
    fn = final_norm.reshape(1, d)

    for i in range(depth):
        sh_a, sc_a, g_a, sh_f, sc_f, g_f = [
            mod[i, :, t * d:(t + 1) * d].reshape(b, 1, d) for t in range(6)]
        j = i // 2
        if i % 2 == 0:
            n_in = mla_w_in.shape[2]
            w_in = jnp.pad(mla_w_in[j], ((0, 0), (0, -n_in % LANES))).astype(BF16)
            hd = MLA_NOPE + MLA_ROPE
            w_q = jnp.pad(mla_w_q_up[j].reshape(MLA_Q_LORA, MLA_HEADS, hd),
                          ((0, 0), (0, 0), (0, QK_PAD - hd)))
            w_q = w_q.reshape(MLA_Q_LORA, MLA_HEADS * QK_PAD).astype(BF16)
            q, k, v = _mla_pre(x, sc_a, sh_a, w_in, mla_q_norm[j].reshape(1, -1), w_q,
                               mla_kv_norm[j].reshape(1, -1), mla_w_kv_up[j].astype(BF16), tabs)
            u = _attention(q, k, v)
            w_out = mla_w_out[j].astype(BF16)
        else:
            ng = 2 * ML_HEADS
            n_main = 2 * dk_all + 2 * dv_all
            w_main = ml_w_in[j][:, :n_main].astype(BF16)
            w_gt = ml_w_in[j][:, n_main:].T.astype(BF16)
            w_g = jnp.pad(w_gt.T, ((0, 0), (0, LANES - ng)))
            b_g = jnp.pad(ml_b_gates[j], (0, LANES - ng)).reshape(1, LANES)
            b_gt = ml_b_gates[j].reshape(ng, 1)
            q, k, v, og, gc, gr = _ml_pre(x, sc_a, sh_a, w_main, w_g, w_gt, b_g, b_gt,
                                          dk_all, dv_all)
            u = _ml_scan(q, k, v, og, gc, gr, ml_head_norm[j].reshape(1, dv_all))
            w_out = ml_w_out[j].astype(BF16)
        x = _post_ffn(x, u, w_out, g_a, sc_f, sh_f, g_f, ffn_w_up[i].astype(BF16),
                      ffn_conv_w[i], ffn_conv_b[i].reshape(1, dff),
                      ffn_w_down[i].astype(BF16), fn, final=(i == depth - 1))
    return x
```

```python
import functools

import jax
import jax.numpy as jnp
from jax import lax
from jax.experimental import pallas as pl
from jax.experimental.pallas import tpu as pltpu

EPS = 1e-6
ROPE_THETA = 10000.0
CHUNK = 64
MLA_HEADS = 8
MLA_Q_LORA = 512
MLA_KV_LORA = 256
MLA_NOPE = 128
MLA_ROPE = 64
MLA_V = 128
ML_HEADS = 4

LANES = 128
SUBLANES = 8
QK_PAD = 256
ML_CHUNK = 256
FFN_TM = 512
FFN_CK = 256
VMEM_LIMIT = 56 * 1024 * 1024

BF16 = jnp.bfloat16
F32 = jnp.float32


def _params(sem):
    return pltpu.CompilerParams(dimension_semantics=sem, vmem_limit_bytes=VMEM_LIMIT)


def _const_spec(shape):
    nd = len(shape)
    return pl.BlockSpec(shape, lambda *_: (0,) * nd, pipeline_mode=pl.Buffered(1))


def _rms(x):
    return x * lax.rsqrt(jnp.mean(x * x, axis=-1, keepdims=True) + EPS)


def _mod_kernel(c_ref, w_ref, b_ref, o_ref):
    c = c_ref[...]
    ca = (c * jax.nn.sigmoid(c)).astype(BF16)
    o_ref[0] = jnp.dot(ca, w_ref[0].astype(BF16), preferred_element_type=F32) + b_ref[0]


def _modulation(c, mod_w, mod_b):
    depth, d, n = mod_w.shape
    b = c.shape[0]
    tn = 1536
    return pl.pallas_call(
        _mod_kernel,
        grid=(depth, n // tn),
        in_specs=[pl.BlockSpec((b, d), lambda i, j: (0, 0)),
                  pl.BlockSpec((1, d, tn), lambda i, j: (i, 0, j)),
                  pl.BlockSpec((1, 1, tn), lambda i, j: (i, 0, j))],
        out_specs=pl.BlockSpec((1, b, tn), lambda i, j: (i, 0, j)),
        out_shape=jax.ShapeDtypeStruct((depth, b, n), F32),
        compiler_params=_params(("parallel", "parallel")),
        name="modulation",
    )(c, mod_w, mod_b.reshape(depth, 1, n))


def _rope_table_kernel(pos_ref, inv_ref, cos_ref, sin_ref, nsin_ref):
    ang = pos_ref[0].astype(F32) * inv_ref[...]
    sin = jnp.sin(ang)
    cos_ref[0] = jnp.cos(ang)
    sin_ref[0] = sin
    nsin_ref[0] = -sin


def _rope_tables(positions):
    b, s = positions.shape
    half = MLA_ROPE // 2
    rows = s * half // LANES
    inv_freq = 1.0 / (ROPE_THETA ** (jnp.arange(0, MLA_ROPE, 2, dtype=F32) / MLA_ROPE))
    inv = jnp.tile(inv_freq, LANES // half).reshape(1, LANES)
    pos = jnp.repeat(positions, half, axis=1).reshape(b, rows, LANES)
    spec = pl.BlockSpec((1, rows, LANES), lambda i: (i, 0, 0))
    shp = jax.ShapeDtypeStruct((b, rows, LANES), F32)
    cos, sin, nsin = pl.pallas_call(
        _rope_table_kernel,
        grid=(b,),
        in_specs=[spec, pl.BlockSpec((1, LANES), lambda i: (0, 0))],
        out_specs=[spec, spec, spec],
        out_shape=[shp, shp, shp],
        compiler_params=_params(("parallel",)),
        name="rope_tables",
    )(pos, inv)
    cos, sin, nsin = (t.reshape(b, s, half) for t in (cos, sin, nsin))
    zeros = lambda w: jnp.zeros((b, s, w), F32)
    ta = jnp.concatenate([cos, cos, zeros(LANES - MLA_ROPE)], axis=-1)
    tb = jnp.concatenate([nsin, zeros(LANES - half)], axis=-1)
    tc = jnp.concatenate([zeros(half), sin, zeros(LANES - MLA_ROPE)], axis=-1)
    return ta, tb, tc


def _rope(g, ta, tb, tc):
    return (g * ta + pltpu.roll(g, LANES - MLA_ROPE // 2, 1) * tb
            + pltpu.roll(g, MLA_ROPE // 2, 1) * tc)


def _mla_pre_kernel(x_ref, sc_ref, sh_ref, win_ref, qn_ref, wq_ref, kvn_ref, wkv_ref,
                    ta_ref, tb_ref, tc_ref, q_ref, k_ref, v_ref):
    h = _rms(x_ref[0]) * (1.0 + sc_ref[0]) + sh_ref[0]
    proj = jnp.dot(h.astype(BF16), win_ref[...], preferred_element_type=F32)
    cq = proj[:, :MLA_Q_LORA]
    ckv = proj[:, MLA_Q_LORA:MLA_Q_LORA + MLA_KV_LORA]
    kr = proj[:, MLA_Q_LORA + MLA_KV_LORA:]
    ta, tb, tc = ta_ref[0], tb_ref[0], tc_ref[0]
    qa = jnp.dot((_rms(cq) * qn_ref[...]).astype(BF16), wq_ref[...],
                 preferred_element_type=F32)
    kva = jnp.dot((_rms(ckv) * kvn_ref[...]).astype(BF16), wkv_ref[...],
                  preferred_element_type=F32)
    krp = _rope(kr, ta, tb, tc).astype(BF16)
    scale = (MLA_NOPE + MLA_ROPE) ** -0.5
    for hd in range(MLA_HEADS):
        o = hd * QK_PAD
        q_ref[0, hd, :, :MLA_NOPE] = (qa[:, o:o + MLA_NOPE] * scale).astype(BF16)
        q_ref[0, hd, :, MLA_NOPE:] = (_rope(qa[:, o + MLA_NOPE:o + QK_PAD], ta, tb, tc)
                                      * scale).astype(BF16)
        k_ref[0, hd, :, :MLA_NOPE] = kva[:, o:o + MLA_NOPE].astype(BF16)
        k_ref[0, hd, :, MLA_NOPE:] = krp
        v_ref[0, hd] = kva[:, o + MLA_NOPE:o + MLA_NOPE + MLA_V].astype(BF16)


def _mla_pre(x, sc, sh, w_in, q_norm, w_q, kv_norm, w_kv, tabs):
    b, s, d = x.shape
    tm = 512
    hq = MLA_HEADS
    row = lambda i, j: (i, j, 0)
    vec = pl.BlockSpec((1, 1, d), lambda i, j: (i, 0, 0))
    tab = pl.BlockSpec((1, tm, LANES), row)
    head_spec = lambda w: pl.BlockSpec((1, hq, tm, w), lambda i, j: (i, 0, j, 0))
    return pl.pallas_call(
        _mla_pre_kernel,
        grid=(b, s // tm),
        in_specs=[pl.BlockSpec((1, tm, d), row), vec, vec,
                  _const_spec(w_in.shape), _const_spec(q_norm.shape), _const_spec(w_q.shape),
                  _const_spec(kv_norm.shape), _const_spec(w_kv.shape), tab, tab, tab],
        out_specs=[head_spec(QK_PAD), head_spec(QK_PAD), head_spec(MLA_V)],
        out_shape=[jax.ShapeDtypeStruct((b, hq, s, QK_PAD), BF16),
                   jax.ShapeDtypeStruct((b, hq, s, QK_PAD), BF16),
                   jax.ShapeDtypeStruct((b, hq, s, MLA_V), BF16)],
        compiler_params=_params(("parallel", "parallel")),
        name="mla_pre",
    )(x, sc, sh, w_in, q_norm, w_q, kv_norm, w_kv, *tabs)


def _attn_kernel(q_ref, k_ref, v_ref, o_ref, *, tq):
    s_len = q_ref.shape[2]
    for i in range(s_len // tq):
        q = q_ref[0, 0, i * tq:(i + 1) * tq, :]
        m = jnp.full((tq, 1), -1e30, F32)
        l = jnp.zeros((tq, 1), F32)
        acc = jnp.zeros((tq, MLA_V), F32)
        for j in range(i + 1):
            kb = k_ref[0, 0, j * tq:(j + 1) * tq, :]
            vb = v_ref[0, 0, j * tq:(j + 1) * tq, :]
            s = lax.dot_general(q, kb, (((1,), (1,)), ((), ())), preferred_element_type=F32)
            if j == i:
                r = lax.broadcasted_iota(jnp.int32, s.shape, 0) // CHUNK
                c = lax.broadcasted_iota(jnp.int32, s.shape, 1) // CHUNK
                s = jnp.where(c <= r, s, -jnp.inf)
            m_new = jnp.maximum(m, jnp.max(s, axis=-1, keepdims=True))
            alpha = jnp.exp(m - m_new)
            p = jnp.exp(s - m_new)
            l = alpha * l + jnp.sum(p, axis=-1, keepdims=True)
            acc = alpha * acc + jnp.dot(p.astype(BF16), vb, preferred_element_type=F32)
            m = m_new
        o_ref[0, i * tq:(i + 1) * tq, :] = (acc / l).astype(o_ref.dtype)


def _attention(q, k, v):
    b, hq, s, _ = q.shape
    tq = 512
    return pl.pallas_call(
        functools.partial(_attn_kernel, tq=tq),
        grid=(b, hq),
        in_specs=[pl.BlockSpec((1, 1, s, QK_PAD), lambda bi, h: (bi, h, 0, 0)),
                  pl.BlockSpec((1, 1, s, QK_PAD), lambda bi, h: (bi, h, 0, 0)),
                  pl.BlockSpec((1, 1, s, MLA_V), lambda bi, h: (bi, h, 0, 0))],
        out_specs=pl.BlockSpec((1, s, MLA_V), lambda bi, h: (bi, 0, h)),
        out_shape=jax.ShapeDtypeStruct((b, s, hq * MLA_V), BF16),
        compiler_params=_params(("parallel", "parallel")),
        name="mla_attention",
    )(q, k, v)


def _log_sigmoid(x):
    return jnp.minimum(x, 0.0) - jnp.log1p(jnp.exp(-jnp.abs(x)))


def _ml_pre_kernel(x_ref, sc_ref, sh_ref, w_ref, wg_ref, wgt_ref, bg_ref, bgt_ref,
                   q_ref, k_ref, v_ref, og_ref, gc_ref, gr_ref, *, dk_all, dv_all):
    h = (_rms(x_ref[0]) * (1.0 + sc_ref[0]) + sh_ref[0]).astype(BF16)
    proj = jnp.dot(h, w_ref[...], preferred_element_type=F32)
    dk = dk_all // ML_HEADS
    q_ref[0] = proj[:, :dk_all].astype(BF16)
    k_ref[0] = (proj[:, dk_all:2 * dk_all] * dk ** -0.5).astype(BF16)
    v_ref[0] = proj[:, 2 * dk_all:2 * dk_all + dv_all].astype(BF16)
    og_ref[0] = proj[:, 2 * dk_all + dv_all:].astype(BF16)
    gcol = jnp.dot(h, wg_ref[...], preferred_element_type=F32) + bg_ref[...]
    lane = lax.broadcasted_iota(jnp.int32, gcol.shape, 1)
    gc_ref[0] = jnp.where(lane < ML_HEADS, gcol, _log_sigmoid(gcol))
    grow = lax.dot_general(wgt_ref[...], h, (((1,), (1,)), ((), ())),
                           preferred_element_type=F32) + bgt_ref[...]
    sub = lax.broadcasted_iota(jnp.int32, grow.shape, 0)
    gr_ref[0] = jnp.where(sub < ML_HEADS, grow, _log_sigmoid(grow))


def _ml_pre(x, sc, sh, w_main, w_g, w_gt, b_g, b_gt, dk_all, dv_all):
    b, s, d = x.shape
    tm = 512
    row = lambda i, j: (i, j, 0)
    vec = pl.BlockSpec((1, 1, d), lambda i, j: (i, 0, 0))
    ng = 2 * ML_HEADS
    return pl.pallas_call(
        functools.partial(_ml_pre_kernel, dk_all=dk_all, dv_all=dv_all),
        grid=(b, s // tm),
        in_specs=[pl.BlockSpec((1, tm, d), row), vec, vec,
                  _const_spec(w_main.shape), _const_spec(w_g.shape), _const_spec(w_gt.shape),
                  _const_spec(b_g.shape), _const_spec(b_gt.shape)],
        out_specs=[pl.BlockSpec((1, tm, dk_all), row), pl.BlockSpec((1, tm, dk_all), row),
                   pl.BlockSpec((1, tm, dv_all), row), pl.BlockSpec((1, tm, dv_all), row),
                   pl.BlockSpec((1, tm, LANES), row),
                   pl.BlockSpec((1, ng, tm), lambda i, j: (i, 0, j))],
        out_shape=[jax.ShapeDtypeStruct((b, s, dk_all), BF16),
                   jax.ShapeDtypeStruct((b, s, dk_all), BF16),
                   jax.ShapeDtypeStruct((b, s, dv_all), BF16),
                   jax.ShapeDtypeStruct((b, s, dv_all), BF16),
                   jax.ShapeDtypeStruct((b, s, LANES), F32),
                   jax.ShapeDtypeStruct((b, ng, s), F32)],
        compiler_params=_params(("parallel", "parallel")),
        name="mlstm_pre",
    )(x, sc, sh, w_main, w_g, w_gt, b_g, b_gt)


def _ml_chunk_prep(gc, gr):
    L = gc.shape[0]
    row = lax.broadcasted_iota(jnp.int32, (L, L), 0)
    col = lax.broadcasted_iota(jnp.int32, (L, L), 1)
    causal = col <= row
    tri = causal.astype(F32)
    tri_t = (row <= col).astype(F32)
    bcol_all = jnp.dot(tri, gc, precision=lax.Precision.HIGHEST, preferred_element_type=F32)
    brow_all = jnp.dot(gr, tri_t, precision=lax.Precision.HIGHEST, preferred_element_type=F32)
    return causal, bcol_all, brow_all


def _ml_head(h, prep, qh, kh, vh, ogh, gc, gr, hn, c_scr, n_scr, m_scr):
    causal, bcol_all, brow_all = prep
    L = qh.shape[0]
    b_col = bcol_all[:, ML_HEADS + h:ML_HEADS + h + 1]
    i_col = gc[:, h:h + 1]
    b_row = brow_all[ML_HEADS + h:ML_HEADS + h + 1, :]
    i_row = gr[h:h + 1, :]
    m_prev = m_scr[h:h + 1, 0:1]
    n_row = n_scr[h:h + 1, :]

    dmat = jnp.where(causal, b_col - b_row + i_row, -jnp.inf)
    inter_log = b_col + m_prev
    m_t = jnp.maximum(inter_log, jnp.max(dmat, axis=-1, keepdims=True))
    inter_w = jnp.exp(inter_log - m_t)
    s_mat = lax.dot_general(qh, kh, (((1,), (1,)), ((), ())),
                            preferred_element_type=F32) * jnp.exp(dmat - m_t)
    c_state = c_scr[h]
    num = (inter_w * jnp.dot(qh, c_state.astype(BF16), preferred_element_type=F32)
           + jnp.dot(s_mat.astype(BF16), vh, preferred_element_type=F32))
    den = (inter_w * jnp.sum(qh.astype(F32) * n_row, axis=-1, keepdims=True)
           + jnp.sum(s_mat, axis=-1, keepdims=True))
    h_out = num / jnp.maximum(jnp.abs(den), jnp.exp(-m_t))
    y = (_rms(h_out) * hn * jax.nn.sigmoid(ogh.astype(F32))).astype(BF16)

    b_last = b_col[L - 1:L, :]
    w_log = b_last - b_row + i_row
    m_new = jnp.maximum(b_last + m_prev, jnp.max(w_log, axis=-1, keepdims=True))
    decay = jnp.exp(b_last + m_prev - m_new)
    ws_col = jnp.exp(b_last - b_col + i_col - m_new)
    ws_row = jnp.exp(w_log - m_new)
    wv = (ws_col * vh.astype(F32)).astype(BF16)
    c_scr[h] = decay * c_state + lax.dot_general(
        kh, wv, (((0,), (0,)), ((), ())), preferred_element_type=F32)
    ws8 = jnp.broadcast_to(ws_row, (SUBLANES, L)).astype(BF16)
    n_upd = jnp.dot(ws8, kh, preferred_element_type=F32)[0:1, :]
    n_scr[h:h + 1, :] = decay * n_row + n_upd
    m_scr[h:h + 1, :] = jnp.broadcast_to(m_new, (1, m_scr.shape[1]))
    return y


def _ml_state_shapes(dk, dv):
    return [pltpu.VMEM((ML_HEADS, dk, dv), F32), pltpu.VMEM((SUBLANES, dk), F32),
            pltpu.VMEM((SUBLANES, LANES), F32)]


def _ffn_tile(x, u, wo_ref, ga, sc, sh, gf, wup_ref, cw_ref, cb_ref, wdn_ref, a_scr, p_scr,
              interleave=()):
    tm, d = x.shape
    dff = p_scr.shape[1]
    halo = SUBLANES
    ck = FFN_CK
    interleave = dict(interleave)
    stage = [0]

    def emit_items():
        for item in interleave.pop(stage[0], ()):
            item()
        stage[0] += 1

    emit_items()
    y = jnp.dot(u, wo_ref[...], preferred_element_type=F32)
    x1 = x + ga * y
    h = (_rms(x1) * (1.0 + sc) + sh).astype(BF16)
    for c in range(dff // ck):
        lo = c * ck
        emit_items()
        a = jnp.dot(h, wup_ref[:, lo:lo + ck], preferred_element_type=F32)
        g = jnp.dot(h, wup_ref[:, dff + lo:dff + lo + ck], preferred_element_type=F32)
        a_scr[halo:halo + tm, lo:lo + ck] = a
        a1 = a_scr[halo - 1:halo - 1 + tm, lo:lo + ck]
        a2 = a_scr[halo - 2:halo - 2 + tm, lo:lo + ck]
        conv = (cw_ref[0:1, lo:lo + ck] * a2 + cw_ref[1:2, lo:lo + ck] * a1
                + cw_ref[2:3, lo:lo + ck] * a + cb_ref[:, lo:lo + ck])
        gelu = 0.5 * conv * (1.0 + lax.erf(conv * (2.0 ** -0.5)))
        p_scr[:, lo:lo + ck] = (gelu * g).astype(BF16)
        a_scr[0:halo, lo:lo + ck] = a_scr[tm:tm + halo, lo:lo + ck]
    cols = []
    for lo in range(0, d, ck):
        emit_items()
        f = jnp.dot(p_scr[...], wdn_ref[:, lo:lo + ck], preferred_element_type=F32)
        cols.append(x1[:, lo:lo + ck] + gf[:, lo:lo + ck] * f)
    assert not interleave, sorted(interleave)
    return jnp.concatenate(cols, axis=1)


def _ffn_scratch(tm, dff):
    return [pltpu.VMEM((tm + SUBLANES, dff), F32), pltpu.VMEM((tm, dff), BF16)]


def _post_ffn_kernel(x_ref, u_ref, wo_ref, ga_ref, sc_ref, sh_ref, gf_ref, wup_ref, cw_ref,
                     cb_ref, wdn_ref, fn_ref, o_ref, a_scr, p_scr, *, final):
    @pl.when(pl.program_id(1) == 0)
    def _():
        a_scr[0:SUBLANES, :] = jnp.zeros((SUBLANES, a_scr.shape[1]), F32)

    x2 = _ffn_tile(x_ref[0], u_ref[0], wo_ref, ga_ref[0], sc_ref[0], sh_ref[0], gf_ref[0],
                   wup_ref, cw_ref, cb_ref, wdn_ref, a_scr, p_scr)
    if final:
        x2 = _rms(x2) * fn_ref[...]
    o_ref[0] = x2


def _post_ffn(x, u, w_out, g_a, sc_f, sh_f, g_f, w_up, conv_w, conv_b, w_down, final_norm,
              final):
    b, s, d = x.shape
    dff = w_down.shape[0]
    tm = FFN_TM
    row = lambda i, j: (i, j, 0)
    vec = pl.BlockSpec((1, 1, d), lambda i, j: (i, 0, 0))
    return pl.pallas_call(
        functools.partial(_post_ffn_kernel, final=final),
        grid=(b, s // tm),
        in_specs=[pl.BlockSpec((1, tm, d), row), pl.BlockSpec((1, tm, d), row),
                  _const_spec(w_out.shape), vec, vec, vec, vec, _const_spec(w_up.shape),
                  _const_spec(conv_w.shape), _const_spec(conv_b.shape),
                  _const_spec(w_down.shape), _const_spec(final_norm.shape)],
        out_specs=pl.BlockSpec((1, tm, d), row),
        out_shape=jax.ShapeDtypeStruct((b, s, d), F32),
        scratch_shapes=_ffn_scratch(tm, dff),
        compiler_params=_params(("parallel", "arbitrary")),
        name="post_ffn",
    )(x, u, w_out, g_a, sc_f, sh_f, g_f, w_up, conv_w, conv_b, w_down, final_norm)


def _ml_scan_ffn_kernel(x_ref, q_ref, k_ref, v_ref, og_ref, gc_ref, gr_ref, hn_ref, wo_ref,
                        ga_ref, sc_ref, sh_ref, gf_ref, wup_ref, cw_ref, cb_ref, wdn_ref, fn_ref,
                        o_ref, a_scr, p_scr, u_scr, c_scr, n_scr, m_scr,
                        *, final, nt, dk, dv):
    s = pl.program_id(0)
    tm = x_ref.shape[1]

    @pl.when(s == 0)
    def _():
        u_scr[...] = jnp.zeros_like(u_scr)

    @pl.when(s % nt == 0)
    def _():
        c_scr[...] = jnp.zeros_like(c_scr)
        n_scr[...] = jnp.zeros_like(n_scr)
        m_scr[...] = jnp.zeros_like(m_scr)

    @pl.when(jnp.maximum(s - 1, 0) % nt == 0)
    def _():
        a_scr[0:SUBLANES, :] = jnp.zeros((SUBLANES, a_scr.shape[1]), F32)

    L = ML_CHUNK
    preps = {}

    def prep_item(c):
        def run():
            r = slice(c * L, (c + 1) * L)
            preps[c] = _ml_chunk_prep(gc_ref[0, r, :], gr_ref[0, :, r])
        return run

    def head_item(c, h):
        def run():
            r = slice(c * L, (c + 1) * L)
            kq = slice(h * dk, (h + 1) * dk)
            vo = slice(h * dv, (h + 1) * dv)
            u_scr[r, vo] = _ml_head(h, preps[c], q_ref[0, r, kq], k_ref[0, r, kq],
                                    v_ref[0, r, vo], og_ref[0, r, vo], gc_ref[0, r, :],
                                    gr_ref[0, :, r], hn_ref[:, vo], c_scr, n_scr, m_scr)
        return run

    items = [prep_item(0)] + [head_item(0, h) for h in range(ML_HEADS)]
    items += [prep_item(1)] + [head_item(1, h) for h in range(ML_HEADS)]
    n_up = p_scr.shape[1] // FFN_CK
    stages = [0, 1, 3, 5, 7, 9, 10, n_up + 1, n_up + 2, n_up + 3]
    u_prev = u_scr[...]
    x2 = _ffn_tile(x_ref[0], u_prev, wo_ref, ga_ref[0], sc_ref[0], sh_ref[0], gf_ref[0],
                   wup_ref, cw_ref, cb_ref, wdn_ref, a_scr, p_scr,
                   interleave={st: [it] for st, it in zip(stages, items)})
    if final:
        x2 = _rms(x2) * fn_ref[...]
    o_ref[0] = x2


def _ml_scan_ffn(x, q, k, v, og, gc, gr, head_norm, w_out, g_a, sc_f, sh_f, g_f, w_up, conv_w,
                 conv_b, w_down, final_norm, final):
    b, s, d = x.shape
    dff = w_down.shape[0]
    dk_all, dv_all = q.shape[-1], v.shape[-1]
    dk, dv = dk_all // ML_HEADS, dv_all // ML_HEADS
    tm = FFN_TM
    nt = s // tm
    last = b * nt - 1
    prod = lambda t: jnp.minimum(t, last)
    cons = lambda t: jnp.maximum(t - 1, 0)
    row_p = lambda t: (prod(t) // nt, prod(t) % nt, 0)
    row_c = lambda t: (cons(t) // nt, cons(t) % nt, 0)
    vec = pl.BlockSpec((1, 1, d), lambda t: (cons(t) // nt, 0, 0))
    return pl.pallas_call(
        functools.partial(_ml_scan_ffn_kernel, final=final, nt=nt, dk=dk, dv=dv),
        grid=(b * nt + 1,),
        in_specs=[pl.BlockSpec((1, tm, d), row_c),
                  pl.BlockSpec((1, tm, dk_all), row_p), pl.BlockSpec((1, tm, dk_all), row_p),
                  pl.BlockSpec((1, tm, dv_all), row_p), pl.BlockSpec((1, tm, dv_all), row_p),
                  pl.BlockSpec((1, tm, LANES), row_p),
                  pl.BlockSpec((1, 2 * ML_HEADS, tm), lambda t: (prod(t) // nt, 0, prod(t) % nt)),
                  _const_spec(head_norm.shape), _const_spec(w_out.shape), vec, vec, vec, vec,
                  _const_spec(w_up.shape), _const_spec(conv_w.shape), _const_spec(conv_b.shape),
                  _const_spec(w_down.shape), _const_spec(final_norm.shape)],
        out_specs=pl.BlockSpec((1, tm, d), row_c),
        out_shape=jax.ShapeDtypeStruct((b, s, d), F32),
        scratch_shapes=(_ffn_scratch(tm, dff) + [pltpu.VMEM((tm, dv_all), BF16)]
                        + _ml_state_shapes(dk, dv)),
        compiler_params=_params(("arbitrary",)),
        name="mlstm_scan_ffn",
    )(x, q, k, v, og, gc, gr, head_norm, w_out, g_a, sc_f, sh_f, g_f, w_up, conv_w, conv_b,
      w_down, final_norm)


def kernel(x, c, positions, mod_w, mod_b, mla_w_in, mla_q_norm, mla_w_q_up, mla_kv_norm,
           mla_w_kv_up, mla_w_out, ml_w_in, ml_b_gates, ml_head_norm, ml_w_out, ffn_w_up,
           ffn_conv_w, ffn_conv_b, ffn_w_down, final_norm):
    b, s, d = x.shape
    depth = mod_w.shape[0]
    dff = ffn_w_down.shape[1]
    dv_all = ml_w_out.shape[1]
    dk_all = (ml_w_in.shape[2] - 2 * dv_all - 2 * ML_HEADS) // 2

    mod = _modulation(c, mod_w, mod_b)
    tabs = _rope_tables(positions)
    fn = final_norm.reshape(1, d)

    for i in range(depth):
        sh_a, sc_a, g_a, sh_f, sc_f, g_f = [
            mod[i, :, t * d:(t + 1) * d].reshape(b, 1, d) for t in range(6)]
        j = i // 2
        if i % 2 == 0:
            n_in = mla_w_in.shape[2]
            w_in = jnp.pad(mla_w_in[j], ((0, 0), (0, -n_in % LANES))).astype(BF16)
            hd = MLA_NOPE + MLA_ROPE
            w_q = jnp.pad(mla_w_q_up[j].reshape(MLA_Q_LORA, MLA_HEADS, hd),
                          ((0, 0), (0, 0), (0, QK_PAD - hd)))
            w_q = w_q.reshape(MLA_Q_LORA, MLA_HEADS * QK_PAD).astype(BF16)
            q, k, v = _mla_pre(x, sc_a, sh_a, w_in, mla_q_norm[j].reshape(1, -1), w_q,
                               mla_kv_norm[j].reshape(1, -1), mla_w_kv_up[j].astype(BF16), tabs)
            mixer = (_attention(q, k, v),)
            stage, w_out = _post_ffn, mla_w_out[j].astype(BF16)
        else:
            ng = 2 * ML_HEADS
            n_main = 2 * dk_all + 2 * dv_all
            w_main = ml_w_in[j][:, :n_main].astype(BF16)
            w_gt = ml_w_in[j][:, n_main:].T.astype(BF16)
            w_g = jnp.pad(w_gt.T, ((0, 0), (0, LANES - ng)))
            b_g = jnp.pad(ml_b_gates[j], (0, LANES - ng)).reshape(1, LANES)
            b_gt = ml_b_gates[j].reshape(ng, 1)
            q, k, v, og, gc, gr = _ml_pre(x, sc_a, sh_a, w_main, w_g, w_gt, b_g, b_gt,
                                          dk_all, dv_all)
            mixer = (q, k, v, og, gc, gr, ml_head_norm[j].reshape(1, dv_all))
            stage, w_out = _ml_scan_ffn, ml_w_out[j].astype(BF16)
        x = stage(x, *mixer, w_out, g_a, sc_f, sh_f, g_f, ffn_w_up[i].astype(BF16),
                  ffn_conv_w[i], ffn_conv_b[i].reshape(1, dff),
                  ffn_w_down[i].astype(BF16), fn, final=(i == depth - 1))
    return x
```

```python
import functools
import math

import jax
import jax.numpy as jnp
from jax import lax
from jax.experimental import pallas as pl
from jax.experimental.pallas import tpu as pltpu

EPS = 1e-6
ROPE_THETA = 10000.0
CHUNK = 64
MLA_HEADS = 8
MLA_Q_LORA = 512
MLA_KV_LORA = 256
MLA_NOPE = 128
MLA_ROPE = 64
MLA_V = 128
ML_HEADS = 4
N_MOD = 6

LANES = 128
SUBLANES = 8
QK_PAD = 256
ML_CHUNK = 256
FFN_TM = 512
FFN_CK = 256
VMEM_LIMIT = 56 * 1024 * 1024
LOG2E = math.log2(math.e)

BF16 = jnp.bfloat16
F32 = jnp.float32


def _params(sem):
    return pltpu.CompilerParams(dimension_semantics=sem, vmem_limit_bytes=VMEM_LIMIT)


def _const_spec(shape):
    nd = len(shape)
    return pl.BlockSpec(shape, lambda *_: (0,) * nd, pipeline_mode=pl.Buffered(1))


def _layer_spec(arr, layer):
    nd = arr.ndim - 1
    return pl.BlockSpec((None,) + arr.shape[1:], lambda *_: (layer,) + (0,) * nd,
                        pipeline_mode=pl.Buffered(1))


def _mod_spec(mod, layer, which, batch_of):
    d = mod.shape[-1]
    return pl.BlockSpec((None, None, None, 1, d),
                        lambda *ids: (layer, batch_of(*ids), which, 0, 0))


def _rms(x):
    return x * lax.rsqrt(jnp.mean(x * x, axis=-1, keepdims=True) + EPS)


def _mod_kernel(c_ref, w_ref, b_ref, o_ref):
    c = c_ref[...]
    ca = (c * jax.nn.sigmoid(c)).astype(BF16)
    o_ref[0] = jnp.dot(ca, w_ref[0].astype(BF16), preferred_element_type=F32) + b_ref[0]


def _modulation(c, mod_w, mod_b):
    depth, d, n = mod_w.shape
    b = c.shape[0]
    tn = 1536
    return pl.pallas_call(
        _mod_kernel,
        grid=(depth, n // tn),
        in_specs=[pl.BlockSpec((b, d), lambda i, j: (0, 0)),
                  pl.BlockSpec((1, d, tn), lambda i, j: (i, 0, j)),
                  pl.BlockSpec((1, 1, tn), lambda i, j: (i, 0, j))],
        out_specs=pl.BlockSpec((1, b, tn), lambda i, j: (i, 0, j)),
        out_shape=jax.ShapeDtypeStruct((depth, b, n), F32),
        compiler_params=_params(("parallel", "parallel")),
        name="modulation",
    )(c, mod_w, mod_b.reshape(depth, 1, n))


def _rope_table_kernel(pos_ref, inv_ref, cos_ref, sin_ref, nsin_ref):
    ang = pos_ref[0].astype(F32) * inv_ref[...]
    sin = jnp.sin(ang)
    cos_ref[0] = jnp.cos(ang)
    sin_ref[0] = sin
    nsin_ref[0] = -sin


def _rope_tables(positions):
    b, s = positions.shape
    half = MLA_ROPE // 2
    rows = s * half // LANES
    inv_freq = 1.0 / (ROPE_THETA ** (jnp.arange(0, MLA_ROPE, 2, dtype=F32) / MLA_ROPE))
    inv = jnp.tile(inv_freq, LANES // half).reshape(1, LANES)
    pos = jnp.repeat(positions, half, axis=1).reshape(b, rows, LANES)
    spec = pl.BlockSpec((1, rows, LANES), lambda i: (i, 0, 0))
    shp = jax.ShapeDtypeStruct((b, rows, LANES), F32)
    cos, sin, nsin = pl.pallas_call(
        _rope_table_kernel,
        grid=(b,),
        in_specs=[spec, pl.BlockSpec((1, LANES), lambda i: (0, 0))],
        out_specs=[spec, spec, spec],
        out_shape=[shp, shp, shp],
        compiler_params=_params(("parallel",)),
        name="rope_tables",
    )(pos, inv)
    cos, sin, nsin = (t.reshape(b, s, half) for t in (cos, sin, nsin))
    zeros = lambda w: jnp.zeros((b, s, w), F32)
    ta = jnp.concatenate([cos, cos, zeros(LANES - MLA_ROPE)], axis=-1)
    tb = jnp.concatenate([nsin, zeros(LANES - half)], axis=-1)
    tc = jnp.concatenate([zeros(half), sin, zeros(LANES - MLA_ROPE)], axis=-1)
    return ta, tb, tc


def _rope(g, ta, tb, tc):
    return (g * ta + pltpu.roll(g, LANES - MLA_ROPE // 2, 1) * tb
            + pltpu.roll(g, MLA_ROPE // 2, 1) * tc)


def _mla_pre_kernel(x_ref, sc_ref, sh_ref, win_ref, qn_ref, wq_ref, kvn_ref, wkv_ref,
                    ta_ref, tb_ref, tc_ref, q_ref, k_ref, v_ref):
    h = _rms(x_ref[0]) * (1.0 + sc_ref[...]) + sh_ref[...]
    proj = jnp.dot(h.astype(BF16), win_ref[...], preferred_element_type=F32)
    cq = proj[:, :MLA_Q_LORA]
    ckv = proj[:, MLA_Q_LORA:MLA_Q_LORA + MLA_KV_LORA]
    kr = proj[:, MLA_Q_LORA + MLA_KV_LORA:]
    ta, tb, tc = ta_ref[0], tb_ref[0], tc_ref[0]
    qa = jnp.dot((_rms(cq) * qn_ref[...]).astype(BF16), wq_ref[...],
                 preferred_element_type=F32)
    kva = jnp.dot((_rms(ckv) * kvn_ref[...]).astype(BF16), wkv_ref[...],
                  preferred_element_type=F32)
    krp = _rope(kr, ta, tb, tc).astype(BF16)
    scale = (MLA_NOPE + MLA_ROPE) ** -0.5 * LOG2E
    for hd in range(MLA_HEADS):
        o = hd * QK_PAD
        q_ref[0, hd, :, :MLA_NOPE] = (qa[:, o:o + MLA_NOPE] * scale).astype(BF16)
        q_ref[0, hd, :, MLA_NOPE:] = (_rope(qa[:, o + MLA_NOPE:o + QK_PAD], ta, tb, tc)
                                      * scale).astype(BF16)
        k_ref[0, hd, :, :MLA_NOPE] = kva[:, o:o + MLA_NOPE].astype(BF16)
        k_ref[0, hd, :, MLA_NOPE:] = krp
        v_ref[0, hd] = kva[:, o + MLA_NOPE:o + MLA_NOPE + MLA_V].astype(BF16)


def _mla_pre(x, mod, layer, w_in, q_norm, w_q, kv_norm, w_kv, j, tabs):
    b, s, d = x.shape
    tm = 512
    hq = MLA_HEADS
    row = lambda i, t: (i, t, 0)
    batch_of = lambda i, t: i
    tab = pl.BlockSpec((1, tm, LANES), row)
    head_spec = lambda w: pl.BlockSpec((1, hq, tm, w), lambda i, t: (i, 0, t, 0))
    return pl.pallas_call(
        _mla_pre_kernel,
        grid=(b, s // tm),
        in_specs=[pl.BlockSpec((1, tm, d), row),
                  _mod_spec(mod, layer, 1, batch_of), _mod_spec(mod, layer, 0, batch_of),
                  _const_spec(w_in.shape), _layer_spec(q_norm, j), _const_spec(w_q.shape),
                  _layer_spec(kv_norm, j), _layer_spec(w_kv, j), tab, tab, tab],
        out_specs=[head_spec(QK_PAD), head_spec(QK_PAD), head_spec(MLA_V)],
        out_shape=[jax.ShapeDtypeStruct((b, hq, s, QK_PAD), BF16),
                   jax.ShapeDtypeStruct((b, hq, s, QK_PAD), BF16),
                   jax.ShapeDtypeStruct((b, hq, s, MLA_V), BF16)],
        compiler_params=_params(("parallel", "parallel")),
        name="mla_pre",
    )(x, mod, mod, w_in, q_norm, w_q, kv_norm, w_kv, *tabs)


def _attn_kernel(q_ref, k_ref, v_ref, o_ref, *, tq):
    s_len = q_ref.shape[2]
    for i in range(s_len // tq):
        q = q_ref[0, 0, i * tq:(i + 1) * tq, :]
        m = jnp.full((tq, 1), -1e30, F32)
        l = jnp.zeros((tq, 1), F32)
        acc = jnp.zeros((tq, MLA_V), F32)
        for j in range(i + 1):
            kb = k_ref[0, 0, j * tq:(j + 1) * tq, :]
            vb = v_ref[0, 0, j * tq:(j + 1) * tq, :]
            s = lax.dot_general(q, kb, (((1,), (1,)), ((), ())), preferred_element_type=F32)
            if j == i:
                r = lax.broadcasted_iota(jnp.int32, s.shape, 0) // CHUNK
                c = lax.broadcasted_iota(jnp.int32, s.shape, 1) // CHUNK
                s = jnp.where(c <= r, s, -jnp.inf)
            m_new = jnp.maximum(m, jnp.max(s, axis=-1, keepdims=True))
            alpha = jnp.exp2(m - m_new)
            p = jnp.exp2(s - m_new)
            l = alpha * l + jnp.sum(p, axis=-1, keepdims=True)
            acc = alpha * acc + jnp.dot(p.astype(BF16), vb, preferred_element_type=F32)
            m = m_new
        o_ref[0, i * tq:(i + 1) * tq, :] = (acc / l).astype(o_ref.dtype)


def _attention(q, k, v):
    b, hq, s, _ = q.shape
    tq = 256
    return pl.pallas_call(
        functools.partial(_attn_kernel, tq=tq),
        grid=(b, hq),
        in_specs=[pl.BlockSpec((1, 1, s, QK_PAD), lambda bi, h: (bi, h, 0, 0)),
                  pl.BlockSpec((1, 1, s, QK_PAD), lambda bi, h: (bi, h, 0, 0)),
                  pl.BlockSpec((1, 1, s, MLA_V), lambda bi, h: (bi, h, 0, 0))],
        out_specs=pl.BlockSpec((1, s, MLA_V), lambda bi, h: (bi, 0, h)),
        out_shape=jax.ShapeDtypeStruct((b, s, hq * MLA_V), BF16),
        compiler_params=_params(("parallel", "parallel")),
        name="mla_attention",
    )(q, k, v)


def _log_sigmoid(x):
    return jnp.minimum(x, 0.0) - jnp.log1p(jnp.exp(-jnp.abs(x)))


def _ml_pre_kernel(x_ref, sc_ref, sh_ref, w_ref, wgt_ref, bgt_ref,
                   q_ref, k_ref, v_ref, og_ref, gr_ref, *, dk_all, dv_all):
    h = (_rms(x_ref[0]) * (1.0 + sc_ref[...]) + sh_ref[...]).astype(BF16)
    n_main = 2 * dk_all + 2 * dv_all
    proj = jnp.dot(h, w_ref[:, :n_main], preferred_element_type=F32)
    dk = dk_all // ML_HEADS
    q_ref[0] = proj[:, :dk_all].astype(BF16)
    k_ref[0] = (proj[:, dk_all:2 * dk_all] * dk ** -0.5).astype(BF16)
    v_ref[0] = proj[:, 2 * dk_all:2 * dk_all + dv_all].astype(BF16)
    og_ref[0] = proj[:, 2 * dk_all + dv_all:].astype(BF16)
    grow = lax.dot_general(wgt_ref[...], h, (((1,), (1,)), ((), ())),
                           preferred_element_type=F32) + bgt_ref[...]
    sub = lax.broadcasted_iota(jnp.int32, grow.shape, 0)
    gr_ref[0] = jnp.where(sub < ML_HEADS, grow, _log_sigmoid(grow))


def _ml_pre(x, mod, layer, w_in, w_gt, b_gt, j, dk_all, dv_all):
    b, s, d = x.shape
    tm = 512
    row = lambda i, t: (i, t, 0)
    batch_of = lambda i, t: i
    ng = 2 * ML_HEADS
    return pl.pallas_call(
        functools.partial(_ml_pre_kernel, dk_all=dk_all, dv_all=dv_all),
        grid=(b, s // tm),
        in_specs=[pl.BlockSpec((1, tm, d), row),
                  _mod_spec(mod, layer, 1, batch_of), _mod_spec(mod, layer, 0, batch_of),
                  _layer_spec(w_in, j), _layer_spec(w_gt, j), _layer_spec(b_gt, j)],
        out_specs=[pl.BlockSpec((1, tm, dk_all), row), pl.BlockSpec((1, tm, dk_all), row),
                   pl.BlockSpec((1, tm, dv_all), row), pl.BlockSpec((1, tm, dv_all), row),
                   pl.BlockSpec((1, ng, tm), lambda i, t: (i, 0, t))],
        out_shape=[jax.ShapeDtypeStruct((b, s, dk_all), BF16),
                   jax.ShapeDtypeStruct((b, s, dk_all), BF16),
                   jax.ShapeDtypeStruct((b, s, dv_all), BF16),
                   jax.ShapeDtypeStruct((b, s, dv_all), BF16),
                   jax.ShapeDtypeStruct((b, ng, s), F32)],
        compiler_params=_params(("parallel", "parallel")),
        name="mlstm_pre",
    )(x, mod, mod, w_in, w_gt, b_gt)


def _ml_chunk_prep(gr):
    L = gr.shape[1]
    row = lax.broadcasted_iota(jnp.int32, (L, L), 0)
    col = lax.broadcasted_iota(jnp.int32, (L, L), 1)
    g2 = gr * LOG2E
    tri_t = (row <= col).astype(F32)
    cum = jnp.dot(g2, tri_t, precision=lax.Precision.HIGHEST, preferred_element_type=F32)
    return col <= row, col == row, g2, cum


def _ml_head(h, prep, qh, kh, vh, ogh, hn, c_scr, n_scr, m_scr):
    causal, eye, g2, cum = prep
    L = qh.shape[0]
    i_row = g2[h:h + 1, :]
    lf_row = g2[ML_HEADS + h:ML_HEADS + h + 1, :]
    b_row = cum[ML_HEADS + h:ML_HEADS + h + 1, :]
    r_row = i_row - b_row
    m_prev = m_scr[h:h + 1, 0:1]
    n_row = n_scr[h:h + 1, :]

    r_mat = jnp.where(causal, r_row, -jnp.inf)
    mm = jnp.maximum(m_prev, jnp.max(r_mat, axis=-1, keepdims=True))
    b_col = jnp.sum(jnp.where(causal, lf_row, 0.0), axis=-1, keepdims=True)
    r_col = jnp.max(jnp.where(eye, r_row, -jnp.inf), axis=-1, keepdims=True)
    inter_w = jnp.exp2(m_prev - mm)
    s_mat = lax.dot_general(qh, kh, (((1,), (1,)), ((), ())),
                            preferred_element_type=F32) * jnp.exp2(r_mat - mm)
    c_state = c_scr[h]
    num = (inter_w * jnp.dot(qh, c_state.astype(BF16), preferred_element_type=F32)
           + jnp.dot(s_mat.astype(BF16), vh, preferred_element_type=F32))
    den = (inter_w * jnp.sum(qh.astype(F32) * n_row, axis=-1, keepdims=True)
           + jnp.sum(s_mat, axis=-1, keepdims=True))
    h_out = num * (1.0 / jnp.maximum(jnp.abs(den), jnp.exp2(-(b_col + mm))))
    y = (_rms(h_out) * hn * jax.nn.sigmoid(ogh.astype(F32))).astype(BF16)

    b_last = b_row[:, L - 1:L]
    w_log = b_last + r_row
    m_new = jnp.maximum(b_last + m_prev, jnp.max(w_log, axis=-1, keepdims=True))
    decay = jnp.exp2(b_last + m_prev - m_new)
    ws_col = jnp.exp2(r_col + (b_last - m_new))
    ws_row = jnp.exp2(w_log - m_new)
    wv = (ws_col * vh.astype(F32)).astype(BF16)
    c_scr[h] = decay * c_state + lax.dot_general(
        kh, wv, (((0,), (0,)), ((), ())), preferred_element_type=F32)
    ws8 = jnp.broadcast_to(ws_row, (SUBLANES, L)).astype(BF16)
    n_upd = jnp.dot(ws8, kh, preferred_element_type=F32)[0:1, :]
    n_scr[h:h + 1, :] = decay * n_row + n_upd
    m_scr[h:h + 1, :] = jnp.broadcast_to(m_new, (1, m_scr.shape[1]))
    return y


def _ml_state_shapes(dk, dv):
    return [pltpu.VMEM((ML_HEADS, dk, dv), F32), pltpu.VMEM((SUBLANES, dk), F32),
            pltpu.VMEM((SUBLANES, LANES), F32)]


def _ffn_tile(x, u, wo_ref, ga, sc, sh, gf, wup_ref, cw_ref, cb_ref, wdn_ref, a_scr, p_scr,
              interleave=()):
    tm, d = x.shape
    dff = p_scr.shape[1]
    halo = SUBLANES
    ck = FFN_CK
    interleave = dict(interleave)
    stage = [0]

    def emit_items():
        for item in interleave.pop(stage[0], ()):
            item()
        stage[0] += 1

    emit_items()
    y = jnp.dot(u, wo_ref[...], preferred_element_type=F32)
    x1 = x + ga * y
    h = (_rms(x1) * (1.0 + sc) + sh).astype(BF16)
    for c in range(dff // ck):
        lo = c * ck
        emit_items()
        a = jnp.dot(h, wup_ref[:, lo:lo + ck], preferred_element_type=F32)
        g = jnp.dot(h, wup_ref[:, dff + lo:dff + lo + ck], preferred_element_type=F32)
        a_scr[halo:halo + tm, lo:lo + ck] = a
        a1 = a_scr[halo - 1:halo - 1 + tm, lo:lo + ck]
        a2 = a_scr[halo - 2:halo - 2 + tm, lo:lo + ck]
        conv = (cw_ref[0:1, lo:lo + ck] * a2 + cw_ref[1:2, lo:lo + ck] * a1
                + cw_ref[2:3, lo:lo + ck] * a + cb_ref[:, lo:lo + ck])
        gelu = 0.5 * conv * (1.0 + lax.erf(conv * (2.0 ** -0.5)))
        p_scr[:, lo:lo + ck] = (gelu * g).astype(BF16)
        a_scr[0:halo, lo:lo + ck] = a_scr[tm:tm + halo, lo:lo + ck]
    cols = []
    for lo in range(0, d, ck):
        emit_items()
        f = jnp.dot(p_scr[...], wdn_ref[:, lo:lo + ck], preferred_element_type=F32)
        cols.append(x1[:, lo:lo + ck] + gf[:, lo:lo + ck] * f)
    assert not interleave, sorted(interleave)
    return jnp.concatenate(cols, axis=1)


def _ffn_scratch(tm, dff):
    return [pltpu.VMEM((tm + SUBLANES, dff), F32), pltpu.VMEM((tm, dff), BF16)]


def _ffn_specs(mod, layer, batch_of, w_out, j, ffn, final_norm):
    w_up, conv_w, conv_b, w_down = ffn
    return [_layer_spec(w_out, j), _mod_spec(mod, layer, 2, batch_of),
            _mod_spec(mod, layer, 4, batch_of), _mod_spec(mod, layer, 3, batch_of),
            _mod_spec(mod, layer, 5, batch_of), _layer_spec(w_up, layer),
            _layer_spec(conv_w, layer), _layer_spec(conv_b, layer), _layer_spec(w_down, layer),
            _const_spec(final_norm.shape)]


def _post_ffn_kernel(x_ref, u_ref, wo_ref, ga_ref, sc_ref, sh_ref, gf_ref, wup_ref, cw_ref,
                     cb_ref, wdn_ref, fn_ref, o_ref, a_scr, p_scr, *, final):
    @pl.when(pl.program_id(1) == 0)
    def _():
        a_scr[0:SUBLANES, :] = jnp.zeros((SUBLANES, a_scr.shape[1]), F32)

    x2 = _ffn_tile(x_ref[0], u_ref[0], wo_ref, ga_ref[...], sc_ref[...], sh_ref[...],
                   gf_ref[...], wup_ref, cw_ref, cb_ref, wdn_ref, a_scr, p_scr)
    if final:
        x2 = _rms(x2) * fn_ref[...]
    o_ref[0] = x2


def _post_ffn(x, u, mod, layer, w_out, j, ffn, final_norm, final):
    b, s, d = x.shape
    dff = ffn[3].shape[1]
    tm = FFN_TM
    row = lambda i, t: (i, t, 0)
    return pl.pallas_call(
        functools.partial(_post_ffn_kernel, final=final),
        grid=(b, s // tm),
        in_specs=([pl.BlockSpec((1, tm, d), row), pl.BlockSpec((1, tm, d), row)]
                  + _ffn_specs(mod, layer, lambda i, t: i, w_out, j, ffn, final_norm)),
        out_specs=pl.BlockSpec((1, tm, d), row),
        out_shape=jax.ShapeDtypeStruct((b, s, d), F32),
        scratch_shapes=_ffn_scratch(tm, dff),
        compiler_params=_params(("parallel", "arbitrary")),
        name="post_ffn",
    )(x, u, w_out, mod, mod, mod, mod, *ffn, final_norm)


def _ml_scan_ffn_kernel(x_ref, q_ref, k_ref, v_ref, og_ref, gr_ref, hn_ref, wo_ref,
                        ga_ref, sc_ref, sh_ref, gf_ref, wup_ref, cw_ref, cb_ref, wdn_ref, fn_ref,
                        o_ref, a_scr, p_scr, u_scr, c_scr, n_scr, m_scr,
                        *, final, nt, dk, dv):
    s = pl.program_id(0)
    tm = x_ref.shape[1]

    @pl.when(s == 0)
    def _():
        u_scr[...] = jnp.zeros_like(u_scr)

    @pl.when(s % nt == 0)
    def _():
        c_scr[...] = jnp.zeros_like(c_scr)
        n_scr[...] = jnp.zeros_like(n_scr)
        m_scr[...] = jnp.zeros_like(m_scr)

    @pl.when(jnp.maximum(s - 1, 0) % nt == 0)
    def _():
        a_scr[0:SUBLANES, :] = jnp.zeros((SUBLANES, a_scr.shape[1]), F32)

    L = ML_CHUNK
    preps = {}

    def prep_item(c):
        def run():
            preps[c] = _ml_chunk_prep(gr_ref[0, :, c * L:(c + 1) * L])
        return run

    def head_item(c, h):
        def run():
            r = slice(c * L, (c + 1) * L)
            kq = slice(h * dk, (h + 1) * dk)
            vo = slice(h * dv, (h + 1) * dv)
            u_scr[r, vo] = _ml_head(h, preps[c], q_ref[0, r, kq], k_ref[0, r, kq],
                                    v_ref[0, r, vo], og_ref[0, r, vo], hn_ref[:, vo],
                                    c_scr, n_scr, m_scr)
        return run

    items = [prep_item(0)] + [head_item(0, h) for h in range(ML_HEADS)]
    items += [prep_item(1)] + [head_item(1, h) for h in range(ML_HEADS)]
    n_up = p_scr.shape[1] // FFN_CK
    stages = [0, 1, 3, 5, 7, 9, 10, n_up + 1, n_up + 2, n_up + 3]
    u_prev = u_scr[...]
    x2 = _ffn_tile(x_ref[0], u_prev, wo_ref, ga_ref[...], sc_ref[...], sh_ref[...], gf_ref[...],
                   wup_ref, cw_ref, cb_ref, wdn_ref, a_scr, p_scr,
                   interleave={st: [it] for st, it in zip(stages, items)})
    if final:
        x2 = _rms(x2) * fn_ref[...]
    o_ref[0] = x2


def _ml_scan_ffn(x, q, k, v, og, gr, head_norm, mod, layer, w_out, j, ffn, final_norm, final):
    b, s, d = x.shape
    dff = ffn[3].shape[1]
    dk_all, dv_all = q.shape[-1], v.shape[-1]
    dk, dv = dk_all // ML_HEADS, dv_all // ML_HEADS
    tm = FFN_TM
    assert tm == 2 * ML_CHUNK
    nt = s // tm
    last = b * nt - 1
    prod = lambda t: jnp.minimum(t, last)
    cons = lambda t: jnp.maximum(t - 1, 0)
    row_p = lambda t: (prod(t) // nt, prod(t) % nt, 0)
    row_c = lambda t: (cons(t) // nt, cons(t) % nt, 0)
    return pl.pallas_call(
        functools.partial(_ml_scan_ffn_kernel, final=final, nt=nt, dk=dk, dv=dv),
        grid=(b * nt + 1,),
        in_specs=([pl.BlockSpec((1, tm, d), row_c),
                   pl.BlockSpec((1, tm, dk_all), row_p), pl.BlockSpec((1, tm, dk_all), row_p),
                   pl.BlockSpec((1, tm, dv_all), row_p), pl.BlockSpec((1, tm, dv_all), row_p),
                   pl.BlockSpec((1, 2 * ML_HEADS, tm),
                                lambda t: (prod(t) // nt, 0, prod(t) % nt)),
                   _layer_spec(head_norm, j)]
                  + _ffn_specs(mod, layer, lambda t: cons(t) // nt, w_out, j, ffn, final_norm)),
        out_specs=pl.BlockSpec((1, tm, d), row_c),
        out_shape=jax.ShapeDtypeStruct((b, s, d), F32),
        scratch_shapes=(_ffn_scratch(tm, dff) + [pltpu.VMEM((tm, dv_all), BF16)]
                        + _ml_state_shapes(dk, dv)),
        compiler_params=_params(("arbitrary",)),
        name="mlstm_scan_ffn",
    )(x, q, k, v, og, gr, head_norm, w_out, mod, mod, mod, mod, *ffn, final_norm)


def kernel(x, c, positions, mod_w, mod_b, mla_w_in, mla_q_norm, mla_w_q_up, mla_kv_norm,
           mla_w_kv_up, mla_w_out, ml_w_in, ml_b_gates, ml_head_norm, ml_w_out, ffn_w_up,
           ffn_conv_w, ffn_conv_b, ffn_w_down, final_norm):
    b, s, d = x.shape
    depth = mod_w.shape[0]
    dff = ffn_w_down.shape[1]
    dv_all = ml_w_out.shape[1]
    ng = 2 * ML_HEADS
    dk_all = (ml_w_in.shape[2] - 2 * dv_all - ng) // 2
    n_main = 2 * dk_all + 2 * dv_all

    mod = _modulation(c, mod_w, mod_b).reshape(depth, b, N_MOD, 1, d)
    tabs = _rope_tables(positions)
    fn = final_norm.reshape(1, d)

    ffn = (ffn_w_up.astype(BF16), ffn_conv_w, ffn_conv_b.reshape(depth, 1, dff),
           ffn_w_down.astype(BF16))
    n_in = mla_w_in.shape[2]
    mla_w_in_p = jnp.pad(mla_w_in, ((0, 0), (0, 0), (0, -n_in % LANES))).astype(BF16)
    hd = MLA_NOPE + MLA_ROPE
    na = mla_w_q_up.shape[0]
    mla_w_q_p = jnp.pad(mla_w_q_up.reshape(na, MLA_Q_LORA, MLA_HEADS, hd),
                        ((0, 0), (0, 0), (0, 0), (0, QK_PAD - hd)))
    mla_w_q_p = mla_w_q_p.reshape(na, MLA_Q_LORA, MLA_HEADS * QK_PAD).astype(BF16)
    mla_q_norm_r = mla_q_norm.reshape(na, 1, MLA_Q_LORA)
    mla_kv_norm_r = mla_kv_norm.reshape(na, 1, MLA_KV_LORA)
    mla_w_kv_b = mla_w_kv_up.astype(BF16)
    mla_w_out_b = mla_w_out.astype(BF16)
    nb = ml_w_in.shape[0]
    ml_w_in_b = ml_w_in.astype(BF16)
    ml_w_gt = jnp.swapaxes(ml_w_in[:, :, n_main:], 1, 2).astype(BF16)
    ml_b_gt = ml_b_gates.reshape(nb, ng, 1)
    ml_head_norm_r = ml_head_norm.reshape(nb, 1, dv_all)
    ml_w_out_b = ml_w_out.astype(BF16)

    for i in range(depth):
        j = i // 2
        final = i == depth - 1
        if i % 2 == 0:
            q, k, v = _mla_pre(x, mod, i, mla_w_in_p[j], mla_q_norm_r, mla_w_q_p[j],
                               mla_kv_norm_r, mla_w_kv_b, j, tabs)
            u = _attention(q, k, v)
            x = _post_ffn(x, u, mod, i, mla_w_out_b, j, ffn, fn, final)
        else:
            q, k, v, og, gr = _ml_pre(x, mod, i, ml_w_in_b, ml_w_gt, ml_b_gt, j, dk_all, dv_all)
            x = _ml_scan_ffn(x, q, k, v, og, gr, ml_head_norm_r, mod, i, ml_w_out_b, j, ffn, fn,
                             final)
    return x
```

```python
import functools
import math

import jax
import jax.numpy as jnp
from jax import lax
from jax.experimental import pallas as pl
from jax.experimental.pallas import tpu as pltpu

EPS = 1e-6
ROPE_THETA = 10000.0
CHUNK = 64
MLA_HEADS = 8
MLA_Q_LORA = 512
MLA_KV_LORA = 256
MLA_NOPE = 128
MLA_ROPE = 64
MLA_V = 128
ML_HEADS = 4
N_MOD = 6

LANES = 128
SUBLANES = 8
QK_PAD = 256
ML_CHUNK = 256
FFN_TM = 512
FFN_CK = 256
VMEM_LIMIT = 56 * 1024 * 1024
LOG2E = math.log2(math.e)

BF16 = jnp.bfloat16
F32 = jnp.float32


def _params(sem):
    return pltpu.CompilerParams(dimension_semantics=sem, vmem_limit_bytes=VMEM_LIMIT)


def _const_spec(shape):
    nd = len(shape)
    return pl.BlockSpec(shape, lambda *_: (0,) * nd, pipeline_mode=pl.Buffered(1))


def _layer_spec(arr, layer):
    nd = arr.ndim - 1
    return pl.BlockSpec((None,) + arr.shape[1:], lambda *_: (layer,) + (0,) * nd,
                        pipeline_mode=pl.Buffered(1))


def _mod_spec(mod, layer, which, batch_of):
    d = mod.shape[-1]
    return pl.BlockSpec((None, None, None, 1, d),
                        lambda *ids: (layer, batch_of(*ids), which, 0, 0))


def _rms(x):
    return x * lax.rsqrt(jnp.mean(x * x, axis=-1, keepdims=True) + EPS)


def _mod_kernel(c_ref, w_ref, b_ref, o_ref):
    c = c_ref[...]
    ca = (c * jax.nn.sigmoid(c)).astype(BF16)
    o_ref[0] = jnp.dot(ca, w_ref[0].astype(BF16), preferred_element_type=F32) + b_ref[0]


def _modulation(c, mod_w, mod_b):
    depth, d, n = mod_w.shape
    b = c.shape[0]
    tn = 1536
    return pl.pallas_call(
        _mod_kernel,
        grid=(depth, n // tn),
        in_specs=[pl.BlockSpec((b, d), lambda i, j: (0, 0)),
                  pl.BlockSpec((1, d, tn), lambda i, j: (i, 0, j)),
                  pl.BlockSpec((1, 1, tn), lambda i, j: (i, 0, j))],
        out_specs=pl.BlockSpec((1, b, tn), lambda i, j: (i, 0, j)),
        out_shape=jax.ShapeDtypeStruct((depth, b, n), F32),
        compiler_params=_params(("parallel", "parallel")),
        name="modulation",
    )(c, mod_w, mod_b.reshape(depth, 1, n))


def _rope_table_kernel(pos_ref, inv_ref, a_ref, b_ref, c_ref):
    ang = pos_ref[0].astype(F32) * inv_ref[...]
    cos, sin = jnp.cos(ang), jnp.sin(ang)
    rows = ang.shape[0]
    half = MLA_ROPE // 2
    per_row = LANES // half
    lane = lax.broadcasted_iota(jnp.int32, ang.shape, 1)
    first_half = (lane // half) % 2 == 0

    def spread(x, g):
        x = pltpu.roll(x, LANES - half * g, 1) if g else x
        x = jnp.where(lane < half, x, 0.0)
        x = x + pltpu.roll(x, half, 1)
        return x + pltpu.roll(x, 2 * half, 1)

    for g in range(per_row):
        cg, sg = spread(cos, g), spread(sin, g)
        out_rows = pl.ds(g, rows, stride=per_row)
        a_ref[0, out_rows, :] = cg
        b_ref[0, out_rows, :] = jnp.where(first_half, -sg, 0.0)
        c_ref[0, out_rows, :] = jnp.where(first_half, 0.0, sg)


def _rope_tables(positions):
    b, s = positions.shape
    half = MLA_ROPE // 2
    rows = s * half // LANES
    inv_freq = 1.0 / (ROPE_THETA ** (jnp.arange(0, MLA_ROPE, 2, dtype=F32) / MLA_ROPE))
    inv = jnp.tile(inv_freq, LANES // half).reshape(1, LANES)
    pos = jnp.repeat(positions, half, axis=1).reshape(b, rows, LANES)
    spec = pl.BlockSpec((1, s, LANES), lambda i: (i, 0, 0))
    shp = jax.ShapeDtypeStruct((b, s, LANES), F32)
    return pl.pallas_call(
        _rope_table_kernel,
        grid=(b,),
        in_specs=[pl.BlockSpec((1, rows, LANES), lambda i: (i, 0, 0)),
                  pl.BlockSpec((1, LANES), lambda i: (0, 0))],
        out_specs=[spec, spec, spec],
        out_shape=[shp, shp, shp],
        compiler_params=_params(("parallel",)),
        name="rope_tables",
    )(pos, inv)


def _rope(g, ta, tb, tc):
    return (g * ta + pltpu.roll(g, LANES - MLA_ROPE // 2, 1) * tb
            + pltpu.roll(g, MLA_ROPE // 2, 1) * tc)


def _mla_pre_kernel(x_ref, sc_ref, sh_ref, win_ref, qn_ref, wq_ref, kvn_ref, wkv_ref,
                    ta_ref, tb_ref, tc_ref, q_ref, k_ref, v_ref):
    h = _rms(x_ref[0]) * (1.0 + sc_ref[...]) + sh_ref[...]
    proj = jnp.dot(h.astype(BF16), win_ref[...], preferred_element_type=F32)
    cq = proj[:, :MLA_Q_LORA]
    ckv = proj[:, MLA_Q_LORA:MLA_Q_LORA + MLA_KV_LORA]
    kr = proj[:, MLA_Q_LORA + MLA_KV_LORA:]
    ta, tb, tc = ta_ref[0], tb_ref[0], tc_ref[0]
    qa = jnp.dot((_rms(cq) * qn_ref[...]).astype(BF16), wq_ref[...],
                 preferred_element_type=F32)
    kva = jnp.dot((_rms(ckv) * kvn_ref[...]).astype(BF16), wkv_ref[...],
                  preferred_element_type=F32)
    krp = _rope(kr, ta, tb, tc).astype(BF16)
    scale = (MLA_NOPE + MLA_ROPE) ** -0.5 * LOG2E
    lane = lax.broadcasted_iota(jnp.int32, kr.shape, 1)
    n_nope = MLA_HEADS * MLA_NOPE
    for pair in range(MLA_HEADS // 2):
        lo = n_nope + pair * LANES
        rp = _rope(qa[:, lo:lo + LANES], ta, tb, tc) * scale
        for hd, part in ((2 * pair, rp), (2 * pair + 1, pltpu.roll(rp, MLA_ROPE, 1))):
            q_ref[0, hd, :, MLA_NOPE:] = jnp.where(lane < MLA_ROPE, part, 0.0).astype(BF16)
    for hd in range(MLA_HEADS):
        o = hd * QK_PAD
        q_ref[0, hd, :, :MLA_NOPE] = (qa[:, hd * MLA_NOPE:(hd + 1) * MLA_NOPE]
                                      * scale).astype(BF16)
        k_ref[0, hd, :, :MLA_NOPE] = kva[:, o:o + MLA_NOPE].astype(BF16)
        k_ref[0, hd, :, MLA_NOPE:] = krp
        v_ref[0, hd] = kva[:, o + MLA_NOPE:o + MLA_NOPE + MLA_V].astype(BF16)


def _mla_pre(x, mod, layer, w_in, q_norm, w_q, kv_norm, w_kv, j, tabs):
    b, s, d = x.shape
    tm = 512
    hq = MLA_HEADS
    row = lambda i, t: (i, t, 0)
    batch_of = lambda i, t: i
    tab = pl.BlockSpec((1, tm, LANES), row)
    head_spec = lambda w: pl.BlockSpec((1, hq, tm, w), lambda i, t: (i, 0, t, 0))
    return pl.pallas_call(
        _mla_pre_kernel,
        grid=(b, s // tm),
        in_specs=[pl.BlockSpec((1, tm, d), row),
                  _mod_spec(mod, layer, 1, batch_of), _mod_spec(mod, layer, 0, batch_of),
                  _const_spec(w_in.shape), _layer_spec(q_norm, j), _const_spec(w_q.shape),
                  _layer_spec(kv_norm, j), _layer_spec(w_kv, j), tab, tab, tab],
        out_specs=[head_spec(QK_PAD), head_spec(QK_PAD), head_spec(MLA_V)],
        out_shape=[jax.ShapeDtypeStruct((b, hq, s, QK_PAD), BF16),
                   jax.ShapeDtypeStruct((b, hq, s, QK_PAD), BF16),
                   jax.ShapeDtypeStruct((b, hq, s, MLA_V), BF16)],
        compiler_params=_params(("parallel", "parallel")),
        name="mla_pre",
    )(x, mod, mod, w_in, q_norm, w_q, kv_norm, w_kv, *tabs)


def _attn_kernel(q_ref, k_ref, v_ref, o_ref, *, tq):
    s_len = q_ref.shape[2]
    for i in range(s_len // tq):
        q = q_ref[0, 0, i * tq:(i + 1) * tq, :]
        m = jnp.full((tq, 1), -1e30, F32)
        l = jnp.zeros((tq, 1), F32)
        acc = jnp.zeros((tq, MLA_V), F32)
        for j in range(i + 1):
            kb = k_ref[0, 0, j * tq:(j + 1) * tq, :]
            vb = v_ref[0, 0, j * tq:(j + 1) * tq, :]
            s = lax.dot_general(q, kb, (((1,), (1,)), ((), ())), preferred_element_type=F32)
            if j == i:
                r = lax.broadcasted_iota(jnp.int32, s.shape, 0) // CHUNK
                c = lax.broadcasted_iota(jnp.int32, s.shape, 1) // CHUNK
                s = jnp.where(c <= r, s, -jnp.inf)
            m_new = jnp.maximum(m, jnp.max(s, axis=-1, keepdims=True))
            alpha = jnp.exp2(m - m_new)
            p = jnp.exp2(s - m_new)
            l = alpha * l + jnp.sum(p, axis=-1, keepdims=True)
            acc = alpha * acc + jnp.dot(p.astype(BF16), vb, preferred_element_type=F32)
            m = m_new
        o_ref[0, i * tq:(i + 1) * tq, :] = (acc / l).astype(o_ref.dtype)


def _attention(q, k, v):
    b, hq, s, _ = q.shape
    tq = 256
    return pl.pallas_call(
        functools.partial(_attn_kernel, tq=tq),
        grid=(b, hq),
        in_specs=[pl.BlockSpec((1, 1, s, QK_PAD), lambda bi, h: (bi, h, 0, 0)),
                  pl.BlockSpec((1, 1, s, QK_PAD), lambda bi, h: (bi, h, 0, 0)),
                  pl.BlockSpec((1, 1, s, MLA_V), lambda bi, h: (bi, h, 0, 0))],
        out_specs=pl.BlockSpec((1, s, MLA_V), lambda bi, h: (bi, 0, h)),
        out_shape=jax.ShapeDtypeStruct((b, s, hq * MLA_V), BF16),
        compiler_params=_params(("parallel", "parallel")),
        name="mla_attention",
    )(q, k, v)


def _log_sigmoid(x):
    return jnp.minimum(x, 0.0) - jnp.log1p(jnp.exp(-jnp.abs(x)))


def _ml_pre_kernel(x_ref, sc_ref, sh_ref, w_ref, wgt_ref, bgt_ref,
                   q_ref, k_ref, v_ref, og_ref, gr_ref, *, dk_all, dv_all):
    h = (_rms(x_ref[0]) * (1.0 + sc_ref[...]) + sh_ref[...]).astype(BF16)
    n_main = 2 * dk_all + 2 * dv_all
    proj = jnp.dot(h, w_ref[:, :n_main], preferred_element_type=F32)
    dk = dk_all // ML_HEADS
    q_ref[0] = proj[:, :dk_all].astype(BF16)
    k_ref[0] = (proj[:, dk_all:2 * dk_all] * dk ** -0.5).astype(BF16)
    v_ref[0] = proj[:, 2 * dk_all:2 * dk_all + dv_all].astype(BF16)
    og_ref[0] = proj[:, 2 * dk_all + dv_all:].astype(BF16)
    grow = lax.dot_general(wgt_ref[...], h, (((1,), (1,)), ((), ())),
                           preferred_element_type=F32) + bgt_ref[...]
    sub = lax.broadcasted_iota(jnp.int32, grow.shape, 0)
    gr_ref[0] = jnp.where(sub < ML_HEADS, grow, _log_sigmoid(grow))


def _ml_pre(x, mod, layer, w_in, w_gt, b_gt, j, dk_all, dv_all):
    b, s, d = x.shape
    tm = 512
    row = lambda i, t: (i, t, 0)
    batch_of = lambda i, t: i
    ng = 2 * ML_HEADS
    return pl.pallas_call(
        functools.partial(_ml_pre_kernel, dk_all=dk_all, dv_all=dv_all),
        grid=(b, s // tm),
        in_specs=[pl.BlockSpec((1, tm, d), row),
                  _mod_spec(mod, layer, 1, batch_of), _mod_spec(mod, layer, 0, batch_of),
                  _layer_spec(w_in, j), _layer_spec(w_gt, j), _layer_spec(b_gt, j)],
        out_specs=[pl.BlockSpec((1, tm, dk_all), row), pl.BlockSpec((1, tm, dk_all), row),
                   pl.BlockSpec((1, tm, dv_all), row), pl.BlockSpec((1, tm, dv_all), row),
                   pl.BlockSpec((1, ng, tm), lambda i, t: (i, 0, t))],
        out_shape=[jax.ShapeDtypeStruct((b, s, dk_all), BF16),
                   jax.ShapeDtypeStruct((b, s, dk_all), BF16),
                   jax.ShapeDtypeStruct((b, s, dv_all), BF16),
                   jax.ShapeDtypeStruct((b, s, dv_all), BF16),
                   jax.ShapeDtypeStruct((b, ng, s), F32)],
        compiler_params=_params(("parallel", "parallel")),
        name="mlstm_pre",
    )(x, mod, mod, w_in, w_gt, b_gt)


def _ml_chunk_prep(gr):
    L = gr.shape[1]
    row = lax.broadcasted_iota(jnp.int32, (L, L), 0)
    col = lax.broadcasted_iota(jnp.int32, (L, L), 1)
    g2 = gr * LOG2E
    tri_t = (row <= col).astype(F32)
    cum = jnp.dot(g2, tri_t, precision=lax.Precision.HIGHEST, preferred_element_type=F32)
    return col <= row, col == row, g2, cum


def _ml_head(h, prep, qh, kh, vh, ogh, hn, c_scr, n_scr, m_scr):
    causal, eye, g2, cum = prep
    L = qh.shape[0]
    i_row = g2[h:h + 1, :]
    lf_row = g2[ML_HEADS + h:ML_HEADS + h + 1, :]
    b_row = cum[ML_HEADS + h:ML_HEADS + h + 1, :]
    r_row = i_row - b_row
    m_prev = m_scr[h:h + 1, 0:1]
    n_row = n_scr[h:h + 1, :]

    r_mat = jnp.where(causal, r_row, -jnp.inf)
    mm = jnp.maximum(m_prev, jnp.max(r_mat, axis=-1, keepdims=True))
    b_col = jnp.sum(jnp.where(causal, lf_row, 0.0), axis=-1, keepdims=True)
    r_col = jnp.max(jnp.where(eye, r_row, -jnp.inf), axis=-1, keepdims=True)
    inter_w = jnp.exp2(m_prev - mm)
    s_mat = lax.dot_general(qh, kh, (((1,), (1,)), ((), ())),
                            preferred_element_type=F32) * jnp.exp2(r_mat - mm)
    c_state = c_scr[h]
    num = (inter_w * jnp.dot(qh, c_state.astype(BF16), preferred_element_type=F32)
           + jnp.dot(s_mat.astype(BF16), vh, preferred_element_type=F32))
    den = (inter_w * jnp.sum(qh.astype(F32) * n_row, axis=-1, keepdims=True)
           + jnp.sum(s_mat, axis=-1, keepdims=True))
    h_out = num * (1.0 / jnp.maximum(jnp.abs(den), jnp.exp2(-(b_col + mm))))
    y = (_rms(h_out) * hn * jax.nn.sigmoid(ogh.astype(F32))).astype(BF16)

    b_last = b_row[:, L - 1:L]
    w_log = b_last + r_row
    m_new = jnp.maximum(b_last + m_prev, jnp.max(w_log, axis=-1, keepdims=True))
    decay = jnp.exp2(b_last + m_prev - m_new)
    ws_col = jnp.exp2(r_col + (b_last - m_new))
    ws_row = jnp.exp2(w_log - m_new)
    wv = (ws_col * vh.astype(F32)).astype(BF16)
    c_scr[h] = decay * c_state + lax.dot_general(
        kh, wv, (((0,), (0,)), ((), ())), preferred_element_type=F32)
    ws8 = jnp.broadcast_to(ws_row, (SUBLANES, L)).astype(BF16)
    n_upd = jnp.dot(ws8, kh, preferred_element_type=F32)[0:1, :]
    n_scr[h:h + 1, :] = decay * n_row + n_upd
    m_scr[h:h + 1, :] = jnp.broadcast_to(m_new, (1, m_scr.shape[1]))
    return y


def _ml_state_shapes(dk, dv):
    return [pltpu.VMEM((ML_HEADS, dk, dv), F32), pltpu.VMEM((SUBLANES, dk), F32),
            pltpu.VMEM((SUBLANES, LANES), F32)]


def _ffn_tile(x, u, wo_ref, ga, sc, sh, gf, wup_ref, cw_ref, cb_ref, wdn_ref, a_scr, p_scr,
              interleave=()):
    tm, d = x.shape
    dff = p_scr.shape[1]
    halo = SUBLANES
    ck = FFN_CK
    interleave = dict(interleave)
    stage = [0]

    def emit_items():
        for item in interleave.pop(stage[0], ()):
            item()
        stage[0] += 1

    emit_items()
    y = jnp.dot(u, wo_ref[...], preferred_element_type=F32)
    x1 = x + ga * y
    h = (_rms(x1) * (1.0 + sc) + sh).astype(BF16)
    for c in range(dff // ck):
        lo = c * ck
        emit_items()
        a = jnp.dot(h, wup_ref[:, lo:lo + ck], preferred_element_type=F32)
        g = jnp.dot(h, wup_ref[:, dff + lo:dff + lo + ck], preferred_element_type=F32)
        a_scr[halo:halo + tm, lo:lo + ck] = a
        a1 = a_scr[halo - 1:halo - 1 + tm, lo:lo + ck]
        a2 = a_scr[halo - 2:halo - 2 + tm, lo:lo + ck]
        conv = (cw_ref[0:1, lo:lo + ck] * a2 + cw_ref[1:2, lo:lo + ck] * a1
                + cw_ref[2:3, lo:lo + ck] * a + cb_ref[:, lo:lo + ck])
        gelu = 0.5 * conv * (1.0 + lax.erf(conv * (2.0 ** -0.5)))
        p_scr[:, lo:lo + ck] = (gelu * g).astype(BF16)
        a_scr[0:halo, lo:lo + ck] = a_scr[tm:tm + halo, lo:lo + ck]
    cols = []
    for lo in range(0, d, ck):
        emit_items()
        f = jnp.dot(p_scr[...], wdn_ref[:, lo:lo + ck], preferred_element_type=F32)
        cols.append(x1[:, lo:lo + ck] + gf[:, lo:lo + ck] * f)
    assert not interleave, sorted(interleave)
    return jnp.concatenate(cols, axis=1)


def _ffn_scratch(tm, dff):
    return [pltpu.VMEM((tm + SUBLANES, dff), F32), pltpu.VMEM((tm, dff), BF16)]


def _ffn_specs(mod, layer, batch_of, w_out, j, ffn, final_norm):
    w_up, conv_w, conv_b, w_down = ffn
    return [_layer_spec(w_out, j), _mod_spec(mod, layer, 2, batch_of),
            _mod_spec(mod, layer, 4, batch_of), _mod_spec(mod, layer, 3, batch_of),
            _mod_spec(mod, layer, 5, batch_of), _layer_spec(w_up, layer),
            _layer_spec(conv_w, layer), _layer_spec(conv_b, layer), _layer_spec(w_down, layer),
            _const_spec(final_norm.shape)]


def _post_ffn_kernel(x_ref, u_ref, wo_ref, ga_ref, sc_ref, sh_ref, gf_ref, wup_ref, cw_ref,
                     cb_ref, wdn_ref, fn_ref, o_ref, a_scr, p_scr, *, final):
    @pl.when(pl.program_id(1) == 0)
    def _():
        a_scr[0:SUBLANES, :] = jnp.zeros((SUBLANES, a_scr.shape[1]), F32)

    x2 = _ffn_tile(x_ref[0], u_ref[0], wo_ref, ga_ref[...], sc_ref[...], sh_ref[...],
                   gf_ref[...], wup_ref, cw_ref, cb_ref, wdn_ref, a_scr, p_scr)
    if final:
        x2 = _rms(x2) * fn_ref[...]
    o_ref[0] = x2


def _post_ffn(x, u, mod, layer, w_out, j, ffn, final_norm, final):
    b, s, d = x.shape
    dff = ffn[3].shape[1]
    tm = FFN_TM
    row = lambda i, t: (i, t, 0)
    return pl.pallas_call(
        functools.partial(_post_ffn_kernel, final=final),
        grid=(b, s // tm),
        in_specs=([pl.BlockSpec((1, tm, d), row), pl.BlockSpec((1, tm, d), row)]
                  + _ffn_specs(mod, layer, lambda i, t: i, w_out, j, ffn, final_norm)),
        out_specs=pl.BlockSpec((1, tm, d), row),
        out_shape=jax.ShapeDtypeStruct((b, s, d), F32),
        scratch_shapes=_ffn_scratch(tm, dff),
        compiler_params=_params(("parallel", "arbitrary")),
        name="post_ffn",
    )(x, u, w_out, mod, mod, mod, mod, *ffn, final_norm)


def _ml_scan_ffn_kernel(x_ref, q_ref, k_ref, v_ref, og_ref, gr_ref, hn_ref, wo_ref,
                        ga_ref, sc_ref, sh_ref, gf_ref, wup_ref, cw_ref, cb_ref, wdn_ref, fn_ref,
                        o_ref, a_scr, p_scr, u_scr, c_scr, n_scr, m_scr,
                        *, final, nt, dk, dv):
    s = pl.program_id(0)
    tm = x_ref.shape[1]

    @pl.when(s == 0)
    def _():
        u_scr[...] = jnp.zeros_like(u_scr)

    @pl.when(s % nt == 0)
    def _():
        c_scr[...] = jnp.zeros_like(c_scr)
        n_scr[...] = jnp.zeros_like(n_scr)
        m_scr[...] = jnp.zeros_like(m_scr)

    @pl.when(jnp.maximum(s - 1, 0) % nt == 0)
    def _():
        a_scr[0:SUBLANES, :] = jnp.zeros((SUBLANES, a_scr.shape[1]), F32)

    L = ML_CHUNK
    preps = {}

    def prep_item(c):
        def run():
            preps[c] = _ml_chunk_prep(gr_ref[0, :, c * L:(c + 1) * L])
        return run

    def head_item(c, h):
        def run():
            r = slice(c * L, (c + 1) * L)
            kq = slice(h * dk, (h + 1) * dk)
            vo = slice(h * dv, (h + 1) * dv)
            u_scr[r, vo] = _ml_head(h, preps[c], q_ref[0, r, kq], k_ref[0, r, kq],
                                    v_ref[0, r, vo], og_ref[0, r, vo], hn_ref[:, vo],
                                    c_scr, n_scr, m_scr)
        return run

    items = []
    for c in range(tm // L):
        items += [prep_item(c)] + [head_item(c, h) for h in range(ML_HEADS)]
    n_stages = 1 + p_scr.shape[1] // FFN_CK + x_ref.shape[2] // FFN_CK
    interleave = {}
    for n, item in enumerate(items):
        interleave.setdefault(n * n_stages // len(items), []).append(item)
    u_prev = u_scr[...]
    x2 = _ffn_tile(x_ref[0], u_prev, wo_ref, ga_ref[...], sc_ref[...], sh_ref[...], gf_ref[...],
                   wup_ref, cw_ref, cb_ref, wdn_ref, a_scr, p_scr, interleave=interleave)
    if final:
        x2 = _rms(x2) * fn_ref[...]
    o_ref[0] = x2


def _ml_scan_ffn(x, q, k, v, og, gr, head_norm, mod, layer, w_out, j, ffn, final_norm, final):
    b, s, d = x.shape
    dff = ffn[3].shape[1]
    dk_all, dv_all = q.shape[-1], v.shape[-1]
    dk, dv = dk_all // ML_HEADS, dv_all // ML_HEADS
    tm = FFN_TM
    assert tm % ML_CHUNK == 0
    nt = s // tm
    last = b * nt - 1
    prod = lambda t: jnp.minimum(t, last)
    cons = lambda t: jnp.maximum(t - 1, 0)
    row_p = lambda t: (prod(t) // nt, prod(t) % nt, 0)
    row_c = lambda t: (cons(t) // nt, cons(t) % nt, 0)
    return pl.pallas_call(
        functools.partial(_ml_scan_ffn_kernel, final=final, nt=nt, dk=dk, dv=dv),
        grid=(b * nt + 1,),
        in_specs=([pl.BlockSpec((1, tm, d), row_c),
                   pl.BlockSpec((1, tm, dk_all), row_p), pl.BlockSpec((1, tm, dk_all), row_p),
                   pl.BlockSpec((1, tm, dv_all), row_p), pl.BlockSpec((1, tm, dv_all), row_p),
                   pl.BlockSpec((1, 2 * ML_HEADS, tm),
                                lambda t: (prod(t) // nt, 0, prod(t) % nt)),
                   _layer_spec(head_norm, j)]
                  + _ffn_specs(mod, layer, lambda t: cons(t) // nt, w_out, j, ffn, final_norm)),
        out_specs=pl.BlockSpec((1, tm, d), row_c),
        out_shape=jax.ShapeDtypeStruct((b, s, d), F32),
        scratch_shapes=(_ffn_scratch(tm, dff) + [pltpu.VMEM((tm, dv_all), BF16)]
                        + _ml_state_shapes(dk, dv)),
        compiler_params=_params(("arbitrary",)),
        name="mlstm_scan_ffn",
    )(x, q, k, v, og, gr, head_norm, w_out, mod, mod, mod, mod, *ffn, final_norm)


def kernel(x, c, positions, mod_w, mod_b, mla_w_in, mla_q_norm, mla_w_q_up, mla_kv_norm,
           mla_w_kv_up, mla_w_out, ml_w_in, ml_b_gates, ml_head_norm, ml_w_out, ffn_w_up,
           ffn_conv_w, ffn_conv_b, ffn_w_down, final_norm):
    b, s, d = x.shape
    depth = mod_w.shape[0]
    dff = ffn_w_down.shape[1]
    dv_all = ml_w_out.shape[1]
    ng = 2 * ML_HEADS
    dk_all = (ml_w_in.shape[2] - 2 * dv_all - ng) // 2
    n_main = 2 * dk_all + 2 * dv_all

    mod = _modulation(c, mod_w, mod_b).reshape(depth, b, N_MOD, 1, d)
    tabs = _rope_tables(positions)
    fn = final_norm.reshape(1, d)

    ffn = (ffn_w_up.astype(BF16), ffn_conv_w, ffn_conv_b.reshape(depth, 1, dff),
           ffn_w_down.astype(BF16))
    n_in = mla_w_in.shape[2]
    mla_w_in_p = jnp.pad(mla_w_in, ((0, 0), (0, 0), (0, -n_in % LANES))).astype(BF16)
    na = mla_w_q_up.shape[0]
    w_q = mla_w_q_up.reshape(na, MLA_Q_LORA, MLA_HEADS, MLA_NOPE + MLA_ROPE)
    mla_w_q_p = jnp.concatenate([w_q[..., :MLA_NOPE].reshape(na, MLA_Q_LORA, -1),
                                 w_q[..., MLA_NOPE:].reshape(na, MLA_Q_LORA, -1)],
                                axis=-1).astype(BF16)
    mla_q_norm_r = mla_q_norm.reshape(na, 1, MLA_Q_LORA)
    mla_kv_norm_r = mla_kv_norm.reshape(na, 1, MLA_KV_LORA)
    mla_w_kv_b = mla_w_kv_up.astype(BF16)
    mla_w_out_b = mla_w_out.astype(BF16)
    nb = ml_w_in.shape[0]
    ml_w_in_b = ml_w_in.astype(BF16)
    ml_w_gt = jnp.swapaxes(ml_w_in[:, :, n_main:], 1, 2).astype(BF16)
    ml_b_gt = ml_b_gates.reshape(nb, ng, 1)
    ml_head_norm_r = ml_head_norm.reshape(nb, 1, dv_all)
    ml_w_out_b = ml_w_out.astype(BF16)

    for i in range(depth):
        j = i // 2
        final = i == depth - 1
        if i % 2 == 0:
            q, k, v = _mla_pre(x, mod, i, mla_w_in_p[j], mla_q_norm_r, mla_w_q_p[j],
                               mla_kv_norm_r, mla_w_kv_b, j, tabs)
            u = _attention(q, k, v)
            x = _post_ffn(x, u, mod, i, mla_w_out_b, j, ffn, fn, final)
        else:
            q, k, v, og, gr = _ml_pre(x, mod, i, ml_w_in_b, ml_w_gt, ml_b_gt, j, dk_all, dv_all)
            x = _ml_scan_ffn(x, q, k, v, og, gr, ml_head_norm_r, mod, i, ml_w_out_b, j, ffn, fn,
                             final)
    return x
```

```python
import functools
import math

import jax
import jax.numpy as jnp
from jax import lax
from jax.experimental import pallas as pl
from jax.experimental.pallas import tpu as pltpu

EPS = 1e-6
ROPE_THETA = 10000.0
CHUNK = 64
MLA_HEADS = 8
MLA_Q_LORA = 512
MLA_KV_LORA = 256
MLA_NOPE = 128
MLA_ROPE = 64
MLA_V = 128
ML_HEADS = 4
N_MOD = 6

LANES = 128
SUBLANES = 8
QK_PAD = 256
ML_CHUNK = 256
FFN_TM = 512
FFN_CK = 256
VMEM_LIMIT = 56 * 1024 * 1024
LOG2E = math.log2(math.e)

BF16 = jnp.bfloat16
F32 = jnp.float32


def _params(sem):
    return pltpu.CompilerParams(dimension_semantics=sem, vmem_limit_bytes=VMEM_LIMIT)


def _const_spec(shape):
    nd = len(shape)
    return pl.BlockSpec(shape, lambda *_: (0,) * nd, pipeline_mode=pl.Buffered(1))


def _layer_spec(arr, layer):
    nd = arr.ndim - 1
    return pl.BlockSpec((None,) + arr.shape[1:], lambda *_: (layer,) + (0,) * nd,
                        pipeline_mode=pl.Buffered(1))


def _mod_spec(mod, layer, which, batch_of):
    d = mod.shape[-1]
    return pl.BlockSpec((None, None, None, 1, d),
                        lambda *ids: (layer, batch_of(*ids), which, 0, 0))


def _rms(x):
    return x * lax.rsqrt(jnp.mean(x * x, axis=-1, keepdims=True) + EPS)


def _mod_kernel(c_ref, w_ref, b_ref, o_ref):
    c = c_ref[...]
    ca = (c * jax.nn.sigmoid(c)).astype(BF16)
    o_ref[0] = jnp.dot(ca, w_ref[0].astype(BF16), preferred_element_type=F32) + b_ref[0]


def _modulation(c, mod_w, mod_b):
    depth, d, n = mod_w.shape
    b = c.shape[0]
    tn = 1536
    return pl.pallas_call(
        _mod_kernel,
        grid=(depth, n // tn),
        in_specs=[pl.BlockSpec((b, d), lambda i, j: (0, 0)),
                  pl.BlockSpec((1, d, tn), lambda i, j: (i, 0, j)),
                  pl.BlockSpec((1, 1, tn), lambda i, j: (i, 0, j))],
        out_specs=pl.BlockSpec((1, b, tn), lambda i, j: (i, 0, j)),
        out_shape=jax.ShapeDtypeStruct((depth, b, n), F32),
        compiler_params=_params(("parallel", "parallel")),
        name="modulation",
    )(c, mod_w, mod_b.reshape(depth, 1, n))


def _rope_table_kernel(pos_ref, inv_ref, a_ref, b_ref, c_ref):
    ang = pos_ref[0].astype(F32) * inv_ref[...]
    cos, sin = jnp.cos(ang), jnp.sin(ang)
    rows = ang.shape[0]
    half = MLA_ROPE // 2
    per_row = LANES // half
    lane = lax.broadcasted_iota(jnp.int32, ang.shape, 1)
    first_half = (lane // half) % 2 == 0

    def spread(x, g):
        x = pltpu.roll(x, LANES - half * g, 1) if g else x
        x = jnp.where(lane < half, x, 0.0)
        x = x + pltpu.roll(x, half, 1)
        return x + pltpu.roll(x, 2 * half, 1)

    for g in range(per_row):
        cg, sg = spread(cos, g), spread(sin, g)
        out_rows = pl.ds(g, rows, stride=per_row)
        a_ref[0, out_rows, :] = cg
        b_ref[0, out_rows, :] = jnp.where(first_half, -sg, 0.0)
        c_ref[0, out_rows, :] = jnp.where(first_half, 0.0, sg)


def _rope_tables(positions):
    b, s = positions.shape
    half = MLA_ROPE // 2
    rows = s * half // LANES
    inv_freq = 1.0 / (ROPE_THETA ** (jnp.arange(0, MLA_ROPE, 2, dtype=F32) / MLA_ROPE))
    inv = jnp.tile(inv_freq, LANES // half).reshape(1, LANES)
    pos = jnp.repeat(positions, half, axis=1).reshape(b, rows, LANES)
    spec = pl.BlockSpec((1, s, LANES), lambda i: (i, 0, 0))
    shp = jax.ShapeDtypeStruct((b, s, LANES), F32)
    return pl.pallas_call(
        _rope_table_kernel,
        grid=(b,),
        in_specs=[pl.BlockSpec((1, rows, LANES), lambda i: (i, 0, 0)),
                  pl.BlockSpec((1, LANES), lambda i: (0, 0))],
        out_specs=[spec, spec, spec],
        out_shape=[shp, shp, shp],
        compiler_params=_params(("parallel",)),
        name="rope_tables",
    )(pos, inv)


def _rope(g, ta, tb, tc):
    return (g * ta + pltpu.roll(g, LANES - MLA_ROPE // 2, 1) * tb
            + pltpu.roll(g, MLA_ROPE // 2, 1) * tc)


def _mla_pre_kernel(x_ref, sc_ref, sh_ref, win_ref, qn_ref, wq_ref, kvn_ref, wkv_ref,
                    ta_ref, tb_ref, tc_ref, q_ref, k_ref, v_ref):
    h = _rms(x_ref[0]) * (1.0 + sc_ref[...]) + sh_ref[...]
    proj = jnp.dot(h.astype(BF16), win_ref[...], preferred_element_type=F32)
    cq = proj[:, :MLA_Q_LORA]
    ckv = proj[:, MLA_Q_LORA:MLA_Q_LORA + MLA_KV_LORA]
    kr = proj[:, MLA_Q_LORA + MLA_KV_LORA:]
    ta, tb, tc = ta_ref[0], tb_ref[0], tc_ref[0]
    qa = jnp.dot((_rms(cq) * qn_ref[...]).astype(BF16), wq_ref[...],
                 preferred_element_type=F32)
    kva = jnp.dot((_rms(ckv) * kvn_ref[...]).astype(BF16), wkv_ref[...],
                  preferred_element_type=F32)
    krp = _rope(kr, ta, tb, tc).astype(BF16)
    scale = (MLA_NOPE + MLA_ROPE) ** -0.5 * LOG2E
    lane = lax.broadcasted_iota(jnp.int32, kr.shape, 1)
    n_nope = MLA_HEADS * MLA_NOPE
    for pair in range(MLA_HEADS // 2):
        lo = n_nope + pair * LANES
        rp = _rope(qa[:, lo:lo + LANES], ta, tb, tc) * scale
        for hd, part in ((2 * pair, rp), (2 * pair + 1, pltpu.roll(rp, MLA_ROPE, 1))):
            q_ref[0, hd, :, MLA_NOPE:] = jnp.where(lane < MLA_ROPE, part, 0.0).astype(BF16)
    for hd in range(MLA_HEADS):
        o = hd * QK_PAD
        q_ref[0, hd, :, :MLA_NOPE] = (qa[:, hd * MLA_NOPE:(hd + 1) * MLA_NOPE]
                                      * scale).astype(BF16)
        k_ref[0, hd, :, :MLA_NOPE] = kva[:, o:o + MLA_NOPE].astype(BF16)
        k_ref[0, hd, :, MLA_NOPE:] = krp
        v_ref[0, hd] = kva[:, o + MLA_NOPE:o + MLA_NOPE + MLA_V].astype(BF16)


def _mla_pre(x, mod, layer, w_in, q_norm, w_q, kv_norm, w_kv, j, tabs):
    b, s, d = x.shape
    tm = 1024
    hq = MLA_HEADS
    row = lambda i, t: (i, t, 0)
    batch_of = lambda i, t: i
    tab = pl.BlockSpec((1, tm, LANES), row)
    head_spec = lambda w: pl.BlockSpec((1, hq, tm, w), lambda i, t: (i, 0, t, 0))
    return pl.pallas_call(
        _mla_pre_kernel,
        grid=(b, s // tm),
        in_specs=[pl.BlockSpec((1, tm, d), row),
                  _mod_spec(mod, layer, 1, batch_of), _mod_spec(mod, layer, 0, batch_of),
                  _const_spec(w_in.shape), _layer_spec(q_norm, j), _const_spec(w_q.shape),
                  _layer_spec(kv_norm, j), _layer_spec(w_kv, j), tab, tab, tab],
        out_specs=[head_spec(QK_PAD), head_spec(QK_PAD), head_spec(MLA_V)],
        out_shape=[jax.ShapeDtypeStruct((b, hq, s, QK_PAD), BF16),
                   jax.ShapeDtypeStruct((b, hq, s, QK_PAD), BF16),
                   jax.ShapeDtypeStruct((b, hq, s, MLA_V), BF16)],
        compiler_params=_params(("parallel", "parallel")),
        name="mla_pre",
    )(x, mod, mod, w_in, q_norm, w_q, kv_norm, w_kv, *tabs)


def _attn_kernel(q_ref, k_ref, v_ref, o_ref, *, tq):
    s_len = q_ref.shape[2]
    for i in range(s_len // tq):
        q = q_ref[0, 0, i * tq:(i + 1) * tq, :]
        m = jnp.full((tq, 1), -1e30, F32)
        l = jnp.zeros((tq, 1), F32)
        acc = jnp.zeros((tq, MLA_V), F32)
        for j in range(i + 1):
            kb = k_ref[0, 0, j * tq:(j + 1) * tq, :]
            vb = v_ref[0, 0, j * tq:(j + 1) * tq, :]
            s = lax.dot_general(q, kb, (((1,), (1,)), ((), ())), preferred_element_type=F32)
            if j == i:
                r = lax.broadcasted_iota(jnp.int32, s.shape, 0) // CHUNK
                c = lax.broadcasted_iota(jnp.int32, s.shape, 1) // CHUNK
                s = jnp.where(c <= r, s, -jnp.inf)
            m_new = jnp.maximum(m, jnp.max(s, axis=-1, keepdims=True))
            alpha = jnp.exp2(m - m_new)
            p = jnp.exp2(s - m_new)
            l = alpha * l + jnp.sum(p, axis=-1, keepdims=True)
            acc = alpha * acc + jnp.dot(p.astype(BF16), vb, preferred_element_type=F32)
            m = m_new
        o_ref[0, i * tq:(i + 1) * tq, :] = (acc / l).astype(o_ref.dtype)


def _attention(q, k, v):
    b, hq, s, _ = q.shape
    tq = 256
    return pl.pallas_call(
        functools.partial(_attn_kernel, tq=tq),
        grid=(b, hq),
        in_specs=[pl.BlockSpec((1, 1, s, QK_PAD), lambda bi, h: (bi, h, 0, 0)),
                  pl.BlockSpec((1, 1, s, QK_PAD), lambda bi, h: (bi, h, 0, 0)),
                  pl.BlockSpec((1, 1, s, MLA_V), lambda bi, h: (bi, h, 0, 0))],
        out_specs=pl.BlockSpec((1, s, MLA_V), lambda bi, h: (bi, 0, h)),
        out_shape=jax.ShapeDtypeStruct((b, s, hq * MLA_V), BF16),
        compiler_params=_params(("parallel", "parallel")),
        name="mla_attention",
    )(q, k, v)


def _log_sigmoid(x):
    return jnp.minimum(x, 0.0) - jnp.log1p(jnp.exp(-jnp.abs(x)))


def _ml_pre_kernel(x_ref, sc_ref, sh_ref, w_ref, wgt_ref, bgt_ref,
                   q_ref, k_ref, v_ref, og_ref, gr_ref, *, dk_all, dv_all):
    h = (_rms(x_ref[0]) * (1.0 + sc_ref[...]) + sh_ref[...]).astype(BF16)
    n_main = 2 * dk_all + 2 * dv_all
    proj = jnp.dot(h, w_ref[:, :n_main], preferred_element_type=F32)
    dk = dk_all // ML_HEADS
    q_ref[0] = proj[:, :dk_all].astype(BF16)
    k_ref[0] = (proj[:, dk_all:2 * dk_all] * dk ** -0.5).astype(BF16)
    v_ref[0] = proj[:, 2 * dk_all:2 * dk_all + dv_all].astype(BF16)
    og_ref[0] = proj[:, 2 * dk_all + dv_all:].astype(BF16)
    grow = lax.dot_general(wgt_ref[...], h, (((1,), (1,)), ((), ())),
                           preferred_element_type=F32) + bgt_ref[...]
    sub = lax.broadcasted_iota(jnp.int32, grow.shape, 0)
    gr_ref[0] = jnp.where(sub < ML_HEADS, grow, _log_sigmoid(grow))


def _ml_pre(x, mod, layer, w_in, w_gt, b_gt, j, dk_all, dv_all):
    b, s, d = x.shape
    tm = 1024
    row = lambda i, t: (i, t, 0)
    batch_of = lambda i, t: i
    ng = 2 * ML_HEADS
    return pl.pallas_call(
        functools.partial(_ml_pre_kernel, dk_all=dk_all, dv_all=dv_all),
        grid=(b, s // tm),
        in_specs=[pl.BlockSpec((1, tm, d), row),
                  _mod_spec(mod, layer, 1, batch_of), _mod_spec(mod, layer, 0, batch_of),
                  _layer_spec(w_in, j), _layer_spec(w_gt, j), _layer_spec(b_gt, j)],
        out_specs=[pl.BlockSpec((1, tm, dk_all), row), pl.BlockSpec((1, tm, dk_all), row),
                   pl.BlockSpec((1, tm, dv_all), row), pl.BlockSpec((1, tm, dv_all), row),
                   pl.BlockSpec((1, ng, tm), lambda i, t: (i, 0, t))],
        out_shape=[jax.ShapeDtypeStruct((b, s, dk_all), BF16),
                   jax.ShapeDtypeStruct((b, s, dk_all), BF16),
                   jax.ShapeDtypeStruct((b, s, dv_all), BF16),
                   jax.ShapeDtypeStruct((b, s, dv_all), BF16),
                   jax.ShapeDtypeStruct((b, ng, s), F32)],
        compiler_params=_params(("parallel", "parallel")),
        name="mlstm_pre",
    )(x, mod, mod, w_in, w_gt, b_gt)


def _ml_chunk_prep(gr):
    L = gr.shape[1]
    row = lax.broadcasted_iota(jnp.int32, (L, L), 0)
    col = lax.broadcasted_iota(jnp.int32, (L, L), 1)
    g2 = gr * LOG2E
    lane = lax.broadcasted_iota(jnp.int32, g2.shape, 1)
    cum = g2
    shift = 1
    while shift < L:
        cum = cum + jnp.where(lane >= shift, pltpu.roll(cum, shift, 1), 0.0)
        shift *= 2
    return col <= row, col == row, g2, cum


def _ml_head(h, prep, qh, kh, vh, ogh, hn, c_scr, n_scr, m_scr):
    causal, eye, g2, cum = prep
    L = qh.shape[0]
    i_row = g2[h:h + 1, :]
    lf_row = g2[ML_HEADS + h:ML_HEADS + h + 1, :]
    b_row = cum[ML_HEADS + h:ML_HEADS + h + 1, :]
    r_row = i_row - b_row
    m_prev = m_scr[h:h + 1, 0:1]
    n_row = n_scr[h:h + 1, :]

    r_mat = jnp.where(causal, r_row, -jnp.inf)
    mm = jnp.maximum(m_prev, jnp.max(r_mat, axis=-1, keepdims=True))
    b_col = jnp.sum(jnp.where(causal, lf_row, 0.0), axis=-1, keepdims=True)
    r_col = jnp.max(jnp.where(eye, r_row, -jnp.inf), axis=-1, keepdims=True)
    inter_w = jnp.exp2(m_prev - mm)
    s_mat = lax.dot_general(qh, kh, (((1,), (1,)), ((), ())),
                            preferred_element_type=F32) * jnp.exp2(r_mat - mm)
    c_state = c_scr[h]
    num = (inter_w * jnp.dot(qh, c_state.astype(BF16), preferred_element_type=F32)
           + jnp.dot(s_mat.astype(BF16), vh, preferred_element_type=F32))
    den = (inter_w * jnp.sum(qh.astype(F32) * n_row, axis=-1, keepdims=True)
           + jnp.sum(s_mat, axis=-1, keepdims=True))
    h_out = num * (1.0 / jnp.maximum(jnp.abs(den), jnp.exp2(-(b_col + mm))))
    y = (_rms(h_out) * hn * jax.nn.sigmoid(ogh.astype(F32))).astype(BF16)

    b_last = b_row[:, L - 1:L]
    w_log = b_last + r_row
    m_new = jnp.maximum(b_last + m_prev, jnp.max(w_log, axis=-1, keepdims=True))
    decay = jnp.exp2(b_last + m_prev - m_new)
    ws_col = jnp.exp2(r_col + (b_last - m_new))
    ws_row = jnp.exp2(w_log - m_new)
    wv = (ws_col * vh.astype(F32)).astype(BF16)
    c_scr[h] = decay * c_state + lax.dot_general(
        kh, wv, (((0,), (0,)), ((), ())), preferred_element_type=F32)
    ws8 = jnp.broadcast_to(ws_row, (SUBLANES, L)).astype(BF16)
    n_upd = jnp.dot(ws8, kh, preferred_element_type=F32)[0:1, :]
    n_scr[h:h + 1, :] = decay * n_row + n_upd
    m_scr[h:h + 1, :] = jnp.broadcast_to(m_new, (1, m_scr.shape[1]))
    return y


def _ml_state_shapes(dk, dv):
    return [pltpu.VMEM((ML_HEADS, dk, dv), F32), pltpu.VMEM((SUBLANES, dk), F32),
            pltpu.VMEM((SUBLANES, LANES), F32)]


def _ffn_tile(x, u, wo_ref, ga, sc, sh, gf, wup_ref, cw_ref, cb_ref, wdn_ref, a_scr, p_scr,
              interleave=()):
    tm, d = x.shape
    dff = p_scr.shape[1]
    halo = SUBLANES
    ck = FFN_CK
    interleave = dict(interleave)
    stage = [0]

    def emit_items():
        for item in interleave.pop(stage[0], ()):
            item()
        stage[0] += 1

    emit_items()
    y = jnp.dot(u, wo_ref[...], preferred_element_type=F32)
    x1 = x + ga * y
    h = (_rms(x1) * (1.0 + sc) + sh).astype(BF16)
    for c in range(dff // ck):
        lo = c * ck
        emit_items()
        a = jnp.dot(h, wup_ref[:, lo:lo + ck], preferred_element_type=F32)
        g = jnp.dot(h, wup_ref[:, dff + lo:dff + lo + ck], preferred_element_type=F32)
        a_scr[halo:halo + tm, lo:lo + ck] = a
        a1 = a_scr[halo - 1:halo - 1 + tm, lo:lo + ck]
        a2 = a_scr[halo - 2:halo - 2 + tm, lo:lo + ck]
        conv = (cw_ref[0:1, lo:lo + ck] * a2 + cw_ref[1:2, lo:lo + ck] * a1
                + cw_ref[2:3, lo:lo + ck] * a + cb_ref[:, lo:lo + ck])
        gelu = 0.5 * conv * (1.0 + lax.erf(conv * (2.0 ** -0.5)))
        p_scr[:, lo:lo + ck] = (gelu * g).astype(BF16)
        a_scr[0:halo, lo:lo + ck] = a_scr[tm:tm + halo, lo:lo + ck]
    cols = []
    for lo in range(0, d, ck):
        emit_items()
        f = jnp.dot(p_scr[...], wdn_ref[:, lo:lo + ck], preferred_element_type=F32)
        cols.append(x1[:, lo:lo + ck] + gf[:, lo:lo + ck] * f)
    assert not interleave, sorted(interleave)
    return jnp.concatenate(cols, axis=1)


def _ffn_scratch(tm, dff):
    return [pltpu.VMEM((tm + SUBLANES, dff), F32), pltpu.VMEM((tm, dff), BF16)]


def _ffn_specs(mod, layer, batch_of, w_out, j, ffn, final_norm):
    w_up, conv_w, conv_b, w_down = ffn
    return [_layer_spec(w_out, j), _mod_spec(mod, layer, 2, batch_of),
            _mod_spec(mod, layer, 4, batch_of), _mod_spec(mod, layer, 3, batch_of),
            _mod_spec(mod, layer, 5, batch_of), _layer_spec(w_up, layer),
            _layer_spec(conv_w, layer), _layer_spec(conv_b, layer), _layer_spec(w_down, layer),
            _const_spec(final_norm.shape)]


def _post_ffn_kernel(x_ref, u_ref, wo_ref, ga_ref, sc_ref, sh_ref, gf_ref, wup_ref, cw_ref,
                     cb_ref, wdn_ref, fn_ref, o_ref, a_scr, p_scr, *, final):
    @pl.when(pl.program_id(1) == 0)
    def _():
        a_scr[0:SUBLANES, :] = jnp.zeros((SUBLANES, a_scr.shape[1]), F32)

    x2 = _ffn_tile(x_ref[0], u_ref[0], wo_ref, ga_ref[...], sc_ref[...], sh_ref[...],
                   gf_ref[...], wup_ref, cw_ref, cb_ref, wdn_ref, a_scr, p_scr)
    if final:
        x2 = _rms(x2) * fn_ref[...]
    o_ref[0] = x2


def _post_ffn(x, u, mod, layer, w_out, j, ffn, final_norm, final):
    b, s, d = x.shape
    dff = ffn[3].shape[1]
    tm = FFN_TM
    row = lambda i, t: (i, t, 0)
    return pl.pallas_call(
        functools.partial(_post_ffn_kernel, final=final),
        grid=(b, s // tm),
        in_specs=([pl.BlockSpec((1, tm, d), row), pl.BlockSpec((1, tm, d), row)]
                  + _ffn_specs(mod, layer, lambda i, t: i, w_out, j, ffn, final_norm)),
        out_specs=pl.BlockSpec((1, tm, d), row),
        out_shape=jax.ShapeDtypeStruct((b, s, d), F32),
        scratch_shapes=_ffn_scratch(tm, dff),
        compiler_params=_params(("parallel", "arbitrary")),
        name="post_ffn",
    )(x, u, w_out, mod, mod, mod, mod, *ffn, final_norm)


def _ml_scan_ffn_kernel(x_ref, q_ref, k_ref, v_ref, og_ref, gr_ref, hn_ref, wo_ref,
                        ga_ref, sc_ref, sh_ref, gf_ref, wup_ref, cw_ref, cb_ref, wdn_ref, fn_ref,
                        o_ref, a_scr, p_scr, u_scr, c_scr, n_scr, m_scr,
                        *, final, nt, dk, dv):
    s = pl.program_id(0)
    tm = x_ref.shape[1]

    @pl.when(s == 0)
    def _():
        u_scr[...] = jnp.zeros_like(u_scr)

    @pl.when(s % nt == 0)
    def _():
        c_scr[...] = jnp.zeros_like(c_scr)
        n_scr[...] = jnp.zeros_like(n_scr)
        m_scr[...] = jnp.zeros_like(m_scr)

    @pl.when(jnp.maximum(s - 1, 0) % nt == 0)
    def _():
        a_scr[0:SUBLANES, :] = jnp.zeros((SUBLANES, a_scr.shape[1]), F32)

    L = ML_CHUNK
    preps = {}

    def prep_item(c):
        def run():
            preps[c] = _ml_chunk_prep(gr_ref[0, :, c * L:(c + 1) * L])
        return run

    def head_item(c, h):
        def run():
            r = slice(c * L, (c + 1) * L)
            kq = slice(h * dk, (h + 1) * dk)
            vo = slice(h * dv, (h + 1) * dv)
            u_scr[r, vo] = _ml_head(h, preps[c], q_ref[0, r, kq], k_ref[0, r, kq],
                                    v_ref[0, r, vo], og_ref[0, r, vo], hn_ref[:, vo],
                                    c_scr, n_scr, m_scr)
        return run

    items = []
    for c in range(tm // L):
        items += [prep_item(c)] + [head_item(c, h) for h in range(ML_HEADS)]
    n_stages = 1 + p_scr.shape[1] // FFN_CK + x_ref.shape[2] // FFN_CK
    interleave = {}
    for n, item in enumerate(items):
        interleave.setdefault(n * n_stages // len(items), []).append(item)
    u_prev = u_scr[...]
    x2 = _ffn_tile(x_ref[0], u_prev, wo_ref, ga_ref[...], sc_ref[...], sh_ref[...], gf_ref[...],
                   wup_ref, cw_ref, cb_ref, wdn_ref, a_scr, p_scr, interleave=interleave)
    if final:
        x2 = _rms(x2) * fn_ref[...]
    o_ref[0] = x2


def _ml_scan_ffn(x, q, k, v, og, gr, head_norm, mod, layer, w_out, j, ffn, final_norm, final):
    b, s, d = x.shape
    dff = ffn[3].shape[1]
    dk_all, dv_all = q.shape[-1], v.shape[-1]
    dk, dv = dk_all // ML_HEADS, dv_all // ML_HEADS
    tm = FFN_TM
    assert tm % ML_CHUNK == 0
    nt = s // tm
    last = b * nt - 1
    prod = lambda t: jnp.minimum(t, last)
    cons = lambda t: jnp.maximum(t - 1, 0)
    row_p = lambda t: (prod(t) // nt, prod(t) % nt, 0)
    row_c = lambda t: (cons(t) // nt, cons(t) % nt, 0)
    return pl.pallas_call(
        functools.partial(_ml_scan_ffn_kernel, final=final, nt=nt, dk=dk, dv=dv),
        grid=(b * nt + 1,),
        in_specs=([pl.BlockSpec((1, tm, d), row_c),
                   pl.BlockSpec((1, tm, dk_all), row_p), pl.BlockSpec((1, tm, dk_all), row_p),
                   pl.BlockSpec((1, tm, dv_all), row_p), pl.BlockSpec((1, tm, dv_all), row_p),
                   pl.BlockSpec((1, 2 * ML_HEADS, tm),
                                lambda t: (prod(t) // nt, 0, prod(t) % nt)),
                   _layer_spec(head_norm, j)]
                  + _ffn_specs(mod, layer, lambda t: cons(t) // nt, w_out, j, ffn, final_norm)),
        out_specs=pl.BlockSpec((1, tm, d), row_c),
        out_shape=jax.ShapeDtypeStruct((b, s, d), F32),
        scratch_shapes=(_ffn_scratch(tm, dff) + [pltpu.VMEM((tm, dv_all), BF16)]
                        + _ml_state_shapes(dk, dv)),
        compiler_params=_params(("arbitrary",)),
        name="mlstm_scan_ffn",
    )(x, q, k, v, og, gr, head_norm, w_out, mod, mod, mod, mod, *ffn, final_norm)


def kernel(x, c, positions, mod_w, mod_b, mla_w_in, mla_q_norm, mla_w_q_up, mla_kv_norm,
           mla_w_kv_up, mla_w_out, ml_w_in, ml_b_gates, ml_head_norm, ml_w_out, ffn_w_up,
           ffn_conv_w, ffn_conv_b, ffn_w_down, final_norm):
    b, s, d = x.shape
    depth = mod_w.shape[0]
    dff = ffn_w_down.shape[1]
    dv_all = ml_w_out.shape[1]
    ng = 2 * ML_HEADS
    dk_all = (ml_w_in.shape[2] - 2 * dv_all - ng) // 2
    n_main = 2 * dk_all + 2 * dv_all

    mod = _modulation(c, mod_w, mod_b).reshape(depth, b, N_MOD, 1, d)
    tabs = _rope_tables(positions)
    fn = final_norm.reshape(1, d)

    ffn = (ffn_w_up.astype(BF16), ffn_conv_w, ffn_conv_b.reshape(depth, 1, dff),
           ffn_w_down.astype(BF16))
    n_in = mla_w_in.shape[2]
    mla_w_in_p = jnp.pad(mla_w_in, ((0, 0), (0, 0), (0, -n_in % LANES))).astype(BF16)
    na = mla_w_q_up.shape[0]
    w_q = mla_w_q_up.reshape(na, MLA_Q_LORA, MLA_HEADS, MLA_NOPE + MLA_ROPE)
    mla_w_q_p = jnp.concatenate([w_q[..., :MLA_NOPE].reshape(na, MLA_Q_LORA, -1),
                                 w_q[..., MLA_NOPE:].reshape(na, MLA_Q_LORA, -1)],
                                axis=-1).astype(BF16)
    mla_q_norm_r = mla_q_norm.reshape(na, 1, MLA_Q_LORA)
    mla_kv_norm_r = mla_kv_norm.reshape(na, 1, MLA_KV_LORA)
    mla_w_kv_b = mla_w_kv_up.astype(BF16)
    mla_w_out_b = mla_w_out.astype(BF16)
    nb = ml_w_in.shape[0]
    ml_w_in_b = ml_w_in.astype(BF16)
    ml_w_gt = jnp.swapaxes(ml_w_in[:, :, n_main:], 1, 2).astype(BF16)
    ml_b_gt = ml_b_gates.reshape(nb, ng, 1)
    ml_head_norm_r = ml_head_norm.reshape(nb, 1, dv_all)
    ml_w_out_b = ml_w_out.astype(BF16)

    for i in range(depth):
        j = i // 2
        final = i == depth - 1
        if i % 2 == 0:
            q, k, v = _mla_pre(x, mod, i, mla_w_in_p[j], mla_q_norm_r, mla_w_q_p[j],
                               mla_kv_norm_r, mla_w_kv_b, j, tabs)
            u = _attention(q, k, v)
            x = _post_ffn(x, u, mod, i, mla_w_out_b, j, ffn, fn, final)
        else:
            q, k, v, og, gr = _ml_pre(x, mod, i, ml_w_in_b, ml_w_gt, ml_b_gt, j, dk_all, dv_all)
            x = _ml_scan_ffn(x, q, k, v, og, gr, ml_head_norm_r, mod, i, ml_w_out_b, j, ffn, fn,
                             final)
    return x
```

```python
import functools
import math

import jax
import jax.numpy as jnp
from jax import lax
from jax.experimental import pallas as pl
from jax.experimental.pallas import tpu as pltpu

EPS = 1e-6
ROPE_THETA = 10000.0
CHUNK = 64
MLA_HEADS = 8
MLA_Q_LORA = 512
MLA_KV_LORA = 256
MLA_NOPE = 128
MLA_ROPE = 64
MLA_V = 128
ML_HEADS = 4
N_MOD = 6

LANES = 128
SUBLANES = 8
QK_PAD = 256
ATTN_HEADS_PER_STEP = 2
ML_CHUNK = 256
FFN_TM = 512
FFN_CK = 256
VMEM_LIMIT = 56 * 1024 * 1024
LOG2E = math.log2(math.e)

BF16 = jnp.bfloat16
F32 = jnp.float32


def _params(sem):
    return pltpu.CompilerParams(dimension_semantics=sem, vmem_limit_bytes=VMEM_LIMIT)


def _const_spec(shape):
    nd = len(shape)
    return pl.BlockSpec(shape, lambda *_: (0,) * nd, pipeline_mode=pl.Buffered(1))


def _layer_spec(arr, layer):
    nd = arr.ndim - 1
    return pl.BlockSpec((None,) + arr.shape[1:], lambda *_: (layer,) + (0,) * nd,
                        pipeline_mode=pl.Buffered(1))


def _mod_spec(mod, layer, which, batch_of):
    d = mod.shape[-1]
    return pl.BlockSpec((None, None, None, 1, d),
                        lambda *ids: (layer, batch_of(*ids), which, 0, 0))


def _rms(x):
    return x * lax.rsqrt(jnp.mean(x * x, axis=-1, keepdims=True) + EPS)


def _mod_rope_kernel(c_ref, w_ref, b_ref, pos_ref, inv_ref, mod_ref, ta_ref, tb_ref, tc_ref):
    c = c_ref[...]
    ca = (c * jax.nn.sigmoid(c)).astype(BF16)
    mod_ref[0] = jnp.dot(ca, w_ref[0].astype(BF16), preferred_element_type=F32) + b_ref[0]

    ang = pos_ref[0].astype(F32) * inv_ref[...]
    cos, sin = jnp.cos(ang), jnp.sin(ang)
    rows = ang.shape[0]
    half = MLA_ROPE // 2
    per_row = LANES // half
    lane = lax.broadcasted_iota(jnp.int32, ang.shape, 1)
    first_half = (lane // half) % 2 == 0

    def spread(x, g):
        x = pltpu.roll(x, LANES - half * g, 1) if g else x
        x = jnp.where(lane < half, x, 0.0)
        x = x + pltpu.roll(x, half, 1)
        return x + pltpu.roll(x, 2 * half, 1)

    for g in range(per_row):
        cg, sg = spread(cos, g), spread(sin, g)
        out_rows = pl.ds(g, rows, stride=per_row)
        ta_ref[0, out_rows, :] = cg
        tb_ref[0, out_rows, :] = jnp.where(first_half, -sg, 0.0)
        tc_ref[0, out_rows, :] = jnp.where(first_half, 0.0, sg)


def _modulation_and_rope_tables(c, mod_w, mod_b, positions):
    depth, d, n = mod_w.shape
    b, s = positions.shape
    tn = 1536
    n_col = n // tn
    n_mod = depth * n_col
    half = MLA_ROPE // 2
    rows = s * half // LANES
    inv_freq = 1.0 / (ROPE_THETA ** (jnp.arange(0, MLA_ROPE, 2, dtype=F32) / MLA_ROPE))
    inv = jnp.tile(inv_freq, LANES // half).reshape(1, LANES)
    pos = jnp.repeat(positions, half, axis=1).reshape(b, rows, LANES)
    mod_tile = lambda i: jnp.minimum(i, n_mod - 1)
    seq = lambda i: jnp.minimum(i, b - 1)
    w_idx = lambda i: (mod_tile(i) // n_col, 0, mod_tile(i) % n_col)
    tab = pl.BlockSpec((1, s, LANES), lambda i: (seq(i), 0, 0))
    tab_shape = jax.ShapeDtypeStruct((b, s, LANES), F32)
    mod, ta, tb, tc = pl.pallas_call(
        _mod_rope_kernel,
        grid=(max(n_mod, b),),
        in_specs=[pl.BlockSpec((b, d), lambda i: (0, 0)),
                  pl.BlockSpec((1, d, tn), w_idx),
                  pl.BlockSpec((1, 1, tn), w_idx),
                  pl.BlockSpec((1, rows, LANES), lambda i: (seq(i), 0, 0)),
                  pl.BlockSpec((1, LANES), lambda i: (0, 0))],
        out_specs=[pl.BlockSpec((1, b, tn), w_idx), tab, tab, tab],
        out_shape=[jax.ShapeDtypeStruct((depth, b, n), F32), tab_shape, tab_shape, tab_shape],
        compiler_params=_params(("arbitrary",)),
        name="modulation_rope",
    )(c, mod_w, mod_b.reshape(depth, 1, n), pos, inv)
    return mod, (ta, tb, tc)


def _rope(g, ta, tb, tc):
    return (g * ta + pltpu.roll(g, LANES - MLA_ROPE // 2, 1) * tb
            + pltpu.roll(g, MLA_ROPE // 2, 1) * tc)


def _mla_pre_kernel(x_ref, sc_ref, sh_ref, win_ref, qn_ref, wq_ref, kvn_ref, wkv_ref,
                    ta_ref, tb_ref, tc_ref, q_ref, k_ref, v_ref):
    h = _rms(x_ref[0]) * (1.0 + sc_ref[...]) + sh_ref[...]
    proj = jnp.dot(h.astype(BF16), win_ref[...], preferred_element_type=F32)
    cq = proj[:, :MLA_Q_LORA]
    ckv = proj[:, MLA_Q_LORA:MLA_Q_LORA + MLA_KV_LORA]
    kr = proj[:, MLA_Q_LORA + MLA_KV_LORA:]
    ta, tb, tc = ta_ref[0], tb_ref[0], tc_ref[0]
    qa = jnp.dot((_rms(cq) * qn_ref[...]).astype(BF16), wq_ref[...],
                 preferred_element_type=F32)
    kva = jnp.dot((_rms(ckv) * kvn_ref[...]).astype(BF16), wkv_ref[...],
                  preferred_element_type=F32)
    krp = _rope(kr, ta, tb, tc).astype(BF16)
    scale = (MLA_NOPE + MLA_ROPE) ** -0.5 * LOG2E
    lane = lax.broadcasted_iota(jnp.int32, kr.shape, 1)
    n_nope = MLA_HEADS * MLA_NOPE
    for pair in range(MLA_HEADS // 2):
        lo = n_nope + pair * LANES
        rp = _rope(qa[:, lo:lo + LANES], ta, tb, tc) * scale
        for hd, part in ((2 * pair, rp), (2 * pair + 1, pltpu.roll(rp, MLA_ROPE, 1))):
            q_ref[0, hd, :, MLA_NOPE:] = jnp.where(lane < MLA_ROPE, part, 0.0).astype(BF16)
    for hd in range(MLA_HEADS):
        o = hd * QK_PAD
        q_ref[0, hd, :, :MLA_NOPE] = (qa[:, hd * MLA_NOPE:(hd + 1) * MLA_NOPE]
                                      * scale).astype(BF16)
        k_ref[0, hd, :, :MLA_NOPE] = kva[:, o:o + MLA_NOPE].astype(BF16)
        k_ref[0, hd, :, MLA_NOPE:] = krp
        v_ref[0, hd] = kva[:, o + MLA_NOPE:o + MLA_NOPE + MLA_V].astype(BF16)


def _mla_pre(x, mod, layer, w_in, q_norm, w_q, kv_norm, w_kv, j, tabs):
    b, s, d = x.shape
    tm = 1024
    hq = MLA_HEADS
    row = lambda i, t: (i, t, 0)
    batch_of = lambda i, t: i
    tab = pl.BlockSpec((1, tm, LANES), row)
    head_spec = lambda w: pl.BlockSpec((1, hq, tm, w), lambda i, t: (i, 0, t, 0))
    return pl.pallas_call(
        _mla_pre_kernel,
        grid=(b, s // tm),
        in_specs=[pl.BlockSpec((1, tm, d), row),
                  _mod_spec(mod, layer, 1, batch_of), _mod_spec(mod, layer, 0, batch_of),
                  _const_spec(w_in.shape), _layer_spec(q_norm, j), _const_spec(w_q.shape),
                  _layer_spec(kv_norm, j), _layer_spec(w_kv, j), tab, tab, tab],
        out_specs=[head_spec(QK_PAD), head_spec(QK_PAD), head_spec(MLA_V)],
        out_shape=[jax.ShapeDtypeStruct((b, hq, s, QK_PAD), BF16),
                   jax.ShapeDtypeStruct((b, hq, s, QK_PAD), BF16),
                   jax.ShapeDtypeStruct((b, hq, s, MLA_V), BF16)],
        compiler_params=_params(("parallel", "parallel")),
        name="mla_pre",
    )(x, mod, mod, w_in, q_norm, w_q, kv_norm, w_kv, *tabs)


def _attn_kernel(q_ref, k_ref, v_ref, o_ref, *, tq):
    s_len = q_ref.shape[2]
    for hd in range(q_ref.shape[1]):
        for i in range(s_len // tq):
            q = q_ref[0, hd, i * tq:(i + 1) * tq, :]
            m = jnp.full((tq, 1), -1e30, F32)
            l = jnp.zeros((tq, 1), F32)
            acc = jnp.zeros((tq, MLA_V), F32)
            for j in range(i + 1):
                kb = k_ref[0, hd, j * tq:(j + 1) * tq, :]
                vb = v_ref[0, hd, j * tq:(j + 1) * tq, :]
                s = lax.dot_general(q, kb, (((1,), (1,)), ((), ())),
                                    preferred_element_type=F32)
                if j == i:
                    r = lax.broadcasted_iota(jnp.int32, s.shape, 0) // CHUNK
                    c = lax.broadcasted_iota(jnp.int32, s.shape, 1) // CHUNK
                    s = jnp.where(c <= r, s, -jnp.inf)
                m_new = jnp.maximum(m, jnp.max(s, axis=-1, keepdims=True))
                alpha = jnp.exp2(m - m_new)
                p = jnp.exp2(s - m_new)
                l = alpha * l + jnp.sum(p, axis=-1, keepdims=True)
                acc = alpha * acc + jnp.dot(p.astype(BF16), vb, preferred_element_type=F32)
                m = m_new
            o_ref[0, i * tq:(i + 1) * tq, hd * MLA_V:(hd + 1) * MLA_V] = (
                acc / l).astype(o_ref.dtype)


def _attention(q, k, v):
    b, hq, s, _ = q.shape
    tq = 256
    hs = ATTN_HEADS_PER_STEP
    head_block = lambda w: pl.BlockSpec((1, hs, s, w), lambda bi, h: (bi, h, 0, 0))
    return pl.pallas_call(
        functools.partial(_attn_kernel, tq=tq),
        grid=(b, hq // hs),
        in_specs=[head_block(QK_PAD), head_block(QK_PAD), head_block(MLA_V)],
        out_specs=pl.BlockSpec((1, s, hs * MLA_V), lambda bi, h: (bi, 0, h)),
        out_shape=jax.ShapeDtypeStruct((b, s, hq * MLA_V), BF16),
        compiler_params=_params(("parallel", "parallel")),
        name="mla_attention",
    )(q, k, v)


def _log_sigmoid(x):
    return jnp.minimum(x, 0.0) - jnp.log1p(jnp.exp(-jnp.abs(x)))


def _ml_pre_kernel(x_ref, sc_ref, sh_ref, w_ref, wgt_ref, bgt_ref,
                   q_ref, k_ref, v_ref, og_ref, gr_ref, *, dk_all, dv_all):
    h = (_rms(x_ref[0]) * (1.0 + sc_ref[...]) + sh_ref[...]).astype(BF16)
    n_main = 2 * dk_all + 2 * dv_all
    proj = jnp.dot(h, w_ref[:, :n_main], preferred_element_type=F32)
    dk = dk_all // ML_HEADS
    q_ref[0] = proj[:, :dk_all].astype(BF16)
    k_ref[0] = (proj[:, dk_all:2 * dk_all] * dk ** -0.5).astype(BF16)
    v_ref[0] = proj[:, 2 * dk_all:2 * dk_all + dv_all].astype(BF16)
    og_ref[0] = proj[:, 2 * dk_all + dv_all:].astype(BF16)
    grow = lax.dot_general(wgt_ref[...], h, (((1,), (1,)), ((), ())),
                           preferred_element_type=F32) + bgt_ref[...]
    sub = lax.broadcasted_iota(jnp.int32, grow.shape, 0)
    gr_ref[0] = jnp.where(sub < ML_HEADS, grow, _log_sigmoid(grow))


def _ml_pre(x, mod, layer, w_in, w_gt, b_gt, j, dk_all, dv_all):
    b, s, d = x.shape
    tm = 1024
    row = lambda i, t: (i, t, 0)
    batch_of = lambda i, t: i
    ng = 2 * ML_HEADS
    return pl.pallas_call(
        functools.partial(_ml_pre_kernel, dk_all=dk_all, dv_all=dv_all),
        grid=(b, s // tm),
        in_specs=[pl.BlockSpec((1, tm, d), row),
                  _mod_spec(mod, layer, 1, batch_of), _mod_spec(mod, layer, 0, batch_of),
                  _layer_spec(w_in, j), _layer_spec(w_gt, j), _layer_spec(b_gt, j)],
        out_specs=[pl.BlockSpec((1, tm, dk_all), row), pl.BlockSpec((1, tm, dk_all), row),
                   pl.BlockSpec((1, tm, dv_all), row), pl.BlockSpec((1, tm, dv_all), row),
                   pl.BlockSpec((1, ng, tm), lambda i, t: (i, 0, t))],
        out_shape=[jax.ShapeDtypeStruct((b, s, dk_all), BF16),
                   jax.ShapeDtypeStruct((b, s, dk_all), BF16),
                   jax.ShapeDtypeStruct((b, s, dv_all), BF16),
                   jax.ShapeDtypeStruct((b, s, dv_all), BF16),
                   jax.ShapeDtypeStruct((b, ng, s), F32)],
        compiler_params=_params(("parallel", "parallel")),
        name="mlstm_pre",
    )(x, mod, mod, w_in, w_gt, b_gt)


def _ml_chunk_prep(gr):
    L = gr.shape[1]
    row = lax.broadcasted_iota(jnp.int32, (L, L), 0)
    col = lax.broadcasted_iota(jnp.int32, (L, L), 1)
    g2 = gr * LOG2E
    lane = lax.broadcasted_iota(jnp.int32, g2.shape, 1)
    cum = g2
    shift = 1
    while shift < L:
        cum = cum + jnp.where(lane >= shift, pltpu.roll(cum, shift, 1), 0.0)
        shift *= 2
    return col <= row, col == row, g2, cum


def _ml_head(h, prep, qh, kh, vh, ogh, hn, c_scr, n_scr, m_scr):
    causal, eye, g2, cum = prep
    L = qh.shape[0]
    i_row = g2[h:h + 1, :]
    lf_row = g2[ML_HEADS + h:ML_HEADS + h + 1, :]
    b_row = cum[ML_HEADS + h:ML_HEADS + h + 1, :]
    r_row = i_row - b_row
    m_prev = m_scr[h:h + 1, 0:1]
    n_row = n_scr[h:h + 1, :]

    r_mat = jnp.where(causal, r_row, -jnp.inf)
    mm = jnp.maximum(m_prev, jnp.max(r_mat, axis=-1, keepdims=True))
    b_col = jnp.sum(jnp.where(causal, lf_row, 0.0), axis=-1, keepdims=True)
    r_col = jnp.max(jnp.where(eye, r_row, -jnp.inf), axis=-1, keepdims=True)
    inter_w = jnp.exp2(m_prev - mm)
    s_mat = lax.dot_general(qh, kh, (((1,), (1,)), ((), ())),
                            preferred_element_type=F32) * jnp.exp2(r_mat - mm)
    c_state = c_scr[h]
    num = (inter_w * jnp.dot(qh, c_state.astype(BF16), preferred_element_type=F32)
           + jnp.dot(s_mat.astype(BF16), vh, preferred_element_type=F32))
    den = (inter_w * jnp.sum(qh.astype(F32) * n_row, axis=-1, keepdims=True)
           + jnp.sum(s_mat, axis=-1, keepdims=True))
    h_out = num * (1.0 / jnp.maximum(jnp.abs(den), jnp.exp2(-(b_col + mm))))
    y = (_rms(h_out) * hn * jax.nn.sigmoid(ogh.astype(F32))).astype(BF16)

    b_last = b_row[:, L - 1:L]
    w_log = b_last + r_row
    m_new = jnp.maximum(b_last + m_prev, jnp.max(w_log, axis=-1, keepdims=True))
    decay = jnp.exp2(b_last + m_prev - m_new)
    ws_col = jnp.exp2(r_col + (b_last - m_new))
    ws_row = jnp.exp2(w_log - m_new)
    wv = (ws_col * vh.astype(F32)).astype(BF16)
    c_scr[h] = decay * c_state + lax.dot_general(
        kh, wv, (((0,), (0,)), ((), ())), preferred_element_type=F32)
    ws8 = jnp.broadcast_to(ws_row, (SUBLANES, L)).astype(BF16)
    n_upd = jnp.dot(ws8, kh, preferred_element_type=F32)[0:1, :]
    n_scr[h:h + 1, :] = decay * n_row + n_upd
    m_scr[h:h + 1, :] = jnp.broadcast_to(m_new, (1, m_scr.shape[1]))
    return y


def _ml_state_shapes(dk, dv):
    return [pltpu.VMEM((ML_HEADS, dk, dv), F32), pltpu.VMEM((SUBLANES, dk), F32),
            pltpu.VMEM((SUBLANES, LANES), F32)]


def _ffn_tile(x, u, wo_ref, ga, sc, sh, gf, wup_ref, cw_ref, cb_ref, wdn_ref, a_scr, p_scr,
              interleave=()):
    tm, d = x.shape
    dff = p_scr.shape[1]
    halo = SUBLANES
    ck = FFN_CK
    interleave = dict(interleave)
    stage = [0]

    def emit_items():
        for item in interleave.pop(stage[0], ()):
            item()
        stage[0] += 1

    emit_items()
    y = jnp.dot(u, wo_ref[...], preferred_element_type=F32)
    x1 = x + ga * y
    h = (_rms(x1) * (1.0 + sc) + sh).astype(BF16)
    for c in range(dff // ck):
        lo = c * ck
        emit_items()
        a = jnp.dot(h, wup_ref[:, lo:lo + ck], preferred_element_type=F32)
        g = jnp.dot(h, wup_ref[:, dff + lo:dff + lo + ck], preferred_element_type=F32)
        a_scr[halo:halo + tm, lo:lo + ck] = a
        a1 = a_scr[halo - 1:halo - 1 + tm, lo:lo + ck]
        a2 = a_scr[halo - 2:halo - 2 + tm, lo:lo + ck]
        conv = (cw_ref[0:1, lo:lo + ck] * a2 + cw_ref[1:2, lo:lo + ck] * a1
                + cw_ref[2:3, lo:lo + ck] * a + cb_ref[:, lo:lo + ck])
        gelu = 0.5 * conv * (1.0 + lax.erf(conv * (2.0 ** -0.5)))
        p_scr[:, lo:lo + ck] = (gelu * g).astype(BF16)
        a_scr[0:halo, lo:lo + ck] = a_scr[tm:tm + halo, lo:lo + ck]
    cols = []
    for lo in range(0, d, ck):
        emit_items()
        f = jnp.dot(p_scr[...], wdn_ref[:, lo:lo + ck], preferred_element_type=F32)
        cols.append(x1[:, lo:lo + ck] + gf[:, lo:lo + ck] * f)
    assert not interleave, sorted(interleave)
    return jnp.concatenate(cols, axis=1)


def _ffn_scratch(tm, dff):
    return [pltpu.VMEM((tm + SUBLANES, dff), F32), pltpu.VMEM((tm, dff), BF16)]


def _ffn_specs(mod, layer, batch_of, w_out, j, ffn, final_norm):
    w_up, conv_w, conv_b, w_down = ffn
    return [_layer_spec(w_out, j), _mod_spec(mod, layer, 2, batch_of),
            _mod_spec(mod, layer, 4, batch_of), _mod_spec(mod, layer, 3, batch_of),
            _mod_spec(mod, layer, 5, batch_of), _layer_spec(w_up, layer),
            _layer_spec(conv_w, layer), _layer_spec(conv_b, layer), _layer_spec(w_down, layer),
            _const_spec(final_norm.shape)]


def _post_ffn_kernel(x_ref, u_ref, wo_ref, ga_ref, sc_ref, sh_ref, gf_ref, wup_ref, cw_ref,
                     cb_ref, wdn_ref, fn_ref, o_ref, a_scr, p_scr, *, final):
    @pl.when(pl.program_id(1) == 0)
    def _():
        a_scr[0:SUBLANES, :] = jnp.zeros((SUBLANES, a_scr.shape[1]), F32)

    x2 = _ffn_tile(x_ref[0], u_ref[0], wo_ref, ga_ref[...], sc_ref[...], sh_ref[...],
                   gf_ref[...], wup_ref, cw_ref, cb_ref, wdn_ref, a_scr, p_scr)
    if final:
        x2 = _rms(x2) * fn_ref[...]
    o_ref[0] = x2


def _post_ffn(x, u, mod, layer, w_out, j, ffn, final_norm, final):
    b, s, d = x.shape
    dff = ffn[3].shape[1]
    tm = FFN_TM
    row = lambda i, t: (i, t, 0)
    return pl.pallas_call(
        functools.partial(_post_ffn_kernel, final=final),
        grid=(b, s // tm),
        in_specs=([pl.BlockSpec((1, tm, d), row), pl.BlockSpec((1, tm, d), row)]
                  + _ffn_specs(mod, layer, lambda i, t: i, w_out, j, ffn, final_norm)),
        out_specs=pl.BlockSpec((1, tm, d), row),
        out_shape=jax.ShapeDtypeStruct((b, s, d), F32),
        scratch_shapes=_ffn_scratch(tm, dff),
        compiler_params=_params(("parallel", "arbitrary")),
        name="post_ffn",
    )(x, u, w_out, mod, mod, mod, mod, *ffn, final_norm)


def _ml_scan_ffn_kernel(x_ref, q_ref, k_ref, v_ref, og_ref, gr_ref, hn_ref, wo_ref,
                        ga_ref, sc_ref, sh_ref, gf_ref, wup_ref, cw_ref, cb_ref, wdn_ref, fn_ref,
                        o_ref, a_scr, p_scr, u_scr, c_scr, n_scr, m_scr,
                        *, final, nt, dk, dv):
    s = pl.program_id(0)
    tm = x_ref.shape[1]

    @pl.when(s == 0)
    def _():
        u_scr[...] = jnp.zeros_like(u_scr)

    @pl.when(s % nt == 0)
    def _():
        c_scr[...] = jnp.zeros_like(c_scr)
        n_scr[...] = jnp.zeros_like(n_scr)
        m_scr[...] = jnp.zeros_like(m_scr)

    @pl.when(jnp.maximum(s - 1, 0) % nt == 0)
    def _():
        a_scr[0:SUBLANES, :] = jnp.zeros((SUBLANES, a_scr.shape[1]), F32)

    L = ML_CHUNK
    preps = {}

    def prep_item(c):
        def run():
            preps[c] = _ml_chunk_prep(gr_ref[0, :, c * L:(c + 1) * L])
        return run

    def head_item(c, h):
        def run():
            r = slice(c * L, (c + 1) * L)
            kq = slice(h * dk, (h + 1) * dk)
            vo = slice(h * dv, (h + 1) * dv)
            u_scr[r, vo] = _ml_head(h, preps[c], q_ref[0, r, kq], k_ref[0, r, kq],
                                    v_ref[0, r, vo], og_ref[0, r, vo], hn_ref[:, vo],
                                    c_scr, n_scr, m_scr)
        return run

    items = []
    for c in range(tm // L):
        items += [prep_item(c)] + [head_item(c, h) for h in range(ML_HEADS)]
    n_stages = 1 + p_scr.shape[1] // FFN_CK + x_ref.shape[2] // FFN_CK
    interleave = {}
    for n, item in enumerate(items):
        interleave.setdefault(n * n_stages // len(items), []).append(item)
    u_prev = u_scr[...]
    x2 = _ffn_tile(x_ref[0], u_prev, wo_ref, ga_ref[...], sc_ref[...], sh_ref[...], gf_ref[...],
                   wup_ref, cw_ref, cb_ref, wdn_ref, a_scr, p_scr, interleave=interleave)
    if final:
        x2 = _rms(x2) * fn_ref[...]
    o_ref[0] = x2


def _ml_scan_ffn(x, q, k, v, og, gr, head_norm, mod, layer, w_out, j, ffn, final_norm, final):
    b, s, d = x.shape
    dff = ffn[3].shape[1]
    dk_all, dv_all = q.shape[-1], v.shape[-1]
    dk, dv = dk_all // ML_HEADS, dv_all // ML_HEADS
    tm = FFN_TM
    assert tm % ML_CHUNK == 0
    nt = s // tm
    last = b * nt - 1
    prod = lambda t: jnp.minimum(t, last)
    cons = lambda t: jnp.maximum(t - 1, 0)
    row_p = lambda t: (prod(t) // nt, prod(t) % nt, 0)
    row_c = lambda t: (cons(t) // nt, cons(t) % nt, 0)
    return pl.pallas_call(
        functools.partial(_ml_scan_ffn_kernel, final=final, nt=nt, dk=dk, dv=dv),
        grid=(b * nt + 1,),
        in_specs=([pl.BlockSpec((1, tm, d), row_c),
                   pl.BlockSpec((1, tm, dk_all), row_p), pl.BlockSpec((1, tm, dk_all), row_p),
                   pl.BlockSpec((1, tm, dv_all), row_p), pl.BlockSpec((1, tm, dv_all), row_p),
                   pl.BlockSpec((1, 2 * ML_HEADS, tm),
                                lambda t: (prod(t) // nt, 0, prod(t) % nt)),
                   _layer_spec(head_norm, j)]
                  + _ffn_specs(mod, layer, lambda t: cons(t) // nt, w_out, j, ffn, final_norm)),
        out_specs=pl.BlockSpec((1, tm, d), row_c),
        out_shape=jax.ShapeDtypeStruct((b, s, d), F32),
        scratch_shapes=(_ffn_scratch(tm, dff) + [pltpu.VMEM((tm, dv_all), BF16)]
                        + _ml_state_shapes(dk, dv)),
        compiler_params=_params(("arbitrary",)),
        name="mlstm_scan_ffn",
    )(x, q, k, v, og, gr, head_norm, w_out, mod, mod, mod, mod, *ffn, final_norm)


def kernel(x, c, positions, mod_w, mod_b, mla_w_in, mla_q_norm, mla_w_q_up, mla_kv_norm,
           mla_w_kv_up, mla_w_out, ml_w_in, ml_b_gates, ml_head_norm, ml_w_out, ffn_w_up,
           ffn_conv_w, ffn_conv_b, ffn_w_down, final_norm):
    b, s, d = x.shape
    depth = mod_w.shape[0]
    dff = ffn_w_down.shape[1]
    dv_all = ml_w_out.shape[1]
    ng = 2 * ML_HEADS
    dk_all = (ml_w_in.shape[2] - 2 * dv_all - ng) // 2
    n_main = 2 * dk_all + 2 * dv_all

    mod, tabs = _modulation_and_rope_tables(c, mod_w, mod_b, positions)
    mod = mod.reshape(depth, b, N_MOD, 1, d)
    fn = final_norm.reshape(1, d)

    ffn = (ffn_w_up.astype(BF16), ffn_conv_w, ffn_conv_b.reshape(depth, 1, dff),
           ffn_w_down.astype(BF16))
    n_in = mla_w_in.shape[2]
    mla_w_in_p = jnp.pad(mla_w_in, ((0, 0), (0, 0), (0, -n_in % LANES))).astype(BF16)
    na = mla_w_q_up.shape[0]
    w_q = mla_w_q_up.reshape(na, MLA_Q_LORA, MLA_HEADS, MLA_NOPE + MLA_ROPE)
    mla_w_q_p = jnp.concatenate([w_q[..., :MLA_NOPE].reshape(na, MLA_Q_LORA, -1),
                                 w_q[..., MLA_NOPE:].reshape(na, MLA_Q_LORA, -1)],
                                axis=-1).astype(BF16)
    mla_q_norm_r = mla_q_norm.reshape(na, 1, MLA_Q_LORA)
    mla_kv_norm_r = mla_kv_norm.reshape(na, 1, MLA_KV_LORA)
    mla_w_kv_b = mla_w_kv_up.astype(BF16)
    mla_w_out_b = mla_w_out.astype(BF16)
    nb = ml_w_in.shape[0]
    ml_w_in_b = ml_w_in.astype(BF16)
    ml_w_gt = jnp.swapaxes(ml_w_in[:, :, n_main:], 1, 2).astype(BF16)
    ml_b_gt = ml_b_gates.reshape(nb, ng, 1)
    ml_head_norm_r = ml_head_norm.reshape(nb, 1, dv_all)
    ml_w_out_b = ml_w_out.astype(BF16)

    for i in range(depth):
        j = i // 2
        final = i == depth - 1
        if i % 2 == 0:
            q, k, v = _mla_pre(x, mod, i, mla_w_in_p[j], mla_q_norm_r, mla_w_q_p[j],
                               mla_kv_norm_r, mla_w_kv_b, j, tabs)
            u = _attention(q, k, v)
            x = _post_ffn(x, u, mod, i, mla_w_out_b, j, ffn, fn, final)
        else:
            q, k, v, og, gr = _ml_pre(x, mod, i, ml_w_in_b, ml_w_gt, ml_b_gt, j, dk_all, dv_all)
            x = _ml_scan_ffn(x, q, k, v, og, gr, ml_head_norm_r, mod, i, ml_w_out_b, j, ffn, fn,
                             final)
    return x
```

```python
import functools
import math

import jax
import jax.numpy as jnp
from jax import lax
from jax.experimental import pallas as pl
from jax.experimental.pallas import tpu as pltpu

EPS = 1e-6
ROPE_THETA = 10000.0
CHUNK = 64
MLA_HEADS = 8
MLA_Q_LORA = 512
MLA_KV_LORA = 256
MLA_NOPE = 128
MLA_ROPE = 64
MLA_V = 128
ML_HEADS = 4
N_MOD = 6

LANES = 128
SUBLANES = 8
QK_PAD = 256
ATTN_HEADS_PER_STEP = 4
ML_CHUNK = 256
FFN_TM = 512
FFN_CK = 256
VMEM_LIMIT = 56 * 1024 * 1024
LOG2E = math.log2(math.e)

BF16 = jnp.bfloat16
F32 = jnp.float32


def _params(sem):
    return pltpu.CompilerParams(dimension_semantics=sem, vmem_limit_bytes=VMEM_LIMIT)


def _const_spec(shape):
    nd = len(shape)
    return pl.BlockSpec(shape, lambda *_: (0,) * nd, pipeline_mode=pl.Buffered(1))


def _layer_spec(arr, layer):
    nd = arr.ndim - 1
    return pl.BlockSpec((None,) + arr.shape[1:], lambda *_: (layer,) + (0,) * nd,
                        pipeline_mode=pl.Buffered(1))


def _mod_spec(mod, layer, which, batch_of):
    d = mod.shape[-1]
    return pl.BlockSpec((None, None, None, 1, d),
                        lambda *ids: (layer, batch_of(*ids), which, 0, 0))


def _rms(x):
    return x * lax.rsqrt(jnp.mean(x * x, axis=-1, keepdims=True) + EPS)


def _mod_rope_kernel(c_ref, w_ref, b_ref, pos_ref, inv_ref, mod_ref, cos_ref, sin_ref):
    c = c_ref[...]
    ca = (c * jax.nn.sigmoid(c)).astype(BF16)
    mod_ref[0] = jnp.dot(ca, w_ref[0].astype(BF16), preferred_element_type=F32) + b_ref[0]

    ang = pos_ref[0].astype(F32) * inv_ref[...]
    cos, sin = jnp.cos(ang), jnp.sin(ang)
    rows = ang.shape[0]
    half = MLA_ROPE // 2
    per_row = LANES // half
    lane = lax.broadcasted_iota(jnp.int32, ang.shape, 1)

    def spread(x, g):
        x = pltpu.roll(x, LANES - half * g, 1) if g else x
        x = jnp.where(lane < half, x, 0.0)
        x = x + pltpu.roll(x, half, 1)
        return x + pltpu.roll(x, 2 * half, 1)

    for g in range(per_row):
        out_rows = pl.ds(g, rows, stride=per_row)
        cos_ref[0, out_rows, :] = spread(cos, g)
        sin_ref[0, out_rows, :] = spread(sin, g)


def _modulation_and_rope_tables(c, mod_w, mod_b, positions):
    depth, d, n = mod_w.shape
    b, s = positions.shape
    tn = 1536
    n_col = n // tn
    n_mod = depth * n_col
    half = MLA_ROPE // 2
    rows = s * half // LANES
    inv_freq = 1.0 / (ROPE_THETA ** (jnp.arange(0, MLA_ROPE, 2, dtype=F32) / MLA_ROPE))
    inv = jnp.tile(inv_freq, LANES // half).reshape(1, LANES)
    pos = jnp.repeat(positions, half, axis=1).reshape(b, rows, LANES)
    mod_tile = lambda i: jnp.minimum(i, n_mod - 1)
    seq = lambda i: jnp.minimum(i, b - 1)
    w_idx = lambda i: (mod_tile(i) // n_col, 0, mod_tile(i) % n_col)
    tab = pl.BlockSpec((1, s, LANES), lambda i: (seq(i), 0, 0))
    tab_shape = jax.ShapeDtypeStruct((b, s, LANES), F32)
    mod, cos, sin = pl.pallas_call(
        _mod_rope_kernel,
        grid=(max(n_mod, b),),
        in_specs=[pl.BlockSpec((b, d), lambda i: (0, 0)),
                  pl.BlockSpec((1, d, tn), w_idx),
                  pl.BlockSpec((1, 1, tn), w_idx),
                  pl.BlockSpec((1, rows, LANES), lambda i: (seq(i), 0, 0)),
                  pl.BlockSpec((1, LANES), lambda i: (0, 0))],
        out_specs=[pl.BlockSpec((1, b, tn), w_idx), tab, tab],
        out_shape=[jax.ShapeDtypeStruct((depth, b, n), F32), tab_shape, tab_shape],
        compiler_params=_params(("arbitrary",)),
        name="modulation_rope",
    )(c, mod_w, mod_b.reshape(depth, 1, n), pos, inv)
    return mod, (cos, sin)


def _rope_coeffs(cos, sin):
    lane = lax.broadcasted_iota(jnp.int32, cos.shape, 1)
    first_half = (lane // (MLA_ROPE // 2)) % 2 == 0
    return cos, jnp.where(first_half, -sin, 0.0), jnp.where(first_half, 0.0, sin)


def _rope(g, ta, tb, tc):
    return (g * ta + pltpu.roll(g, LANES - MLA_ROPE // 2, 1) * tb
            + pltpu.roll(g, MLA_ROPE // 2, 1) * tc)


def _mla_pre_kernel(x_ref, sc_ref, sh_ref, win_ref, qn_ref, wq_ref, kvn_ref, wkv_ref,
                    cos_ref, sin_ref, q_ref, kn_ref, kr_ref, v_ref):
    h = _rms(x_ref[0]) * (1.0 + sc_ref[...]) + sh_ref[...]
    proj = jnp.dot(h.astype(BF16), win_ref[...], preferred_element_type=F32)
    cq = proj[:, :MLA_Q_LORA]
    ckv = proj[:, MLA_Q_LORA:MLA_Q_LORA + MLA_KV_LORA]
    kr = proj[:, MLA_Q_LORA + MLA_KV_LORA:]
    ta, tb, tc = _rope_coeffs(cos_ref[0], sin_ref[0])
    qa = jnp.dot((_rms(cq) * qn_ref[...]).astype(BF16), wq_ref[...],
                 preferred_element_type=F32)
    kva = jnp.dot((_rms(ckv) * kvn_ref[...]).astype(BF16), wkv_ref[...],
                  preferred_element_type=F32)
    kr_ref[0] = _rope(kr, ta, tb, tc).astype(BF16)
    scale = (MLA_NOPE + MLA_ROPE) ** -0.5 * LOG2E
    lane = lax.broadcasted_iota(jnp.int32, kr.shape, 1)
    n_nope = MLA_HEADS * MLA_NOPE
    for pair in range(MLA_HEADS // 2):
        lo = n_nope + pair * LANES
        rp = _rope(qa[:, lo:lo + LANES], ta, tb, tc) * scale
        for hd, part in ((2 * pair, rp), (2 * pair + 1, pltpu.roll(rp, MLA_ROPE, 1))):
            q_ref[0, hd, :, MLA_NOPE:] = jnp.where(lane < MLA_ROPE, part, 0.0).astype(BF16)
    for hd in range(MLA_HEADS):
        o = hd * QK_PAD
        q_ref[0, hd, :, :MLA_NOPE] = (qa[:, hd * MLA_NOPE:(hd + 1) * MLA_NOPE]
                                      * scale).astype(BF16)
        kn_ref[0, hd] = kva[:, o:o + MLA_NOPE].astype(BF16)
        v_ref[0, hd] = kva[:, o + MLA_NOPE:o + MLA_NOPE + MLA_V].astype(BF16)


def _mla_pre(x, mod, layer, w_in, q_norm, w_q, kv_norm, w_kv, j, tabs):
    b, s, d = x.shape
    tm = 1024
    hq = MLA_HEADS
    row = lambda i, t: (i, t, 0)
    batch_of = lambda i, t: i
    tab = pl.BlockSpec((1, tm, LANES), row)
    head_spec = lambda w: pl.BlockSpec((1, hq, tm, w), lambda i, t: (i, 0, t, 0))
    return pl.pallas_call(
        _mla_pre_kernel,
        grid=(b, s // tm),
        in_specs=[pl.BlockSpec((1, tm, d), row),
                  _mod_spec(mod, layer, 1, batch_of), _mod_spec(mod, layer, 0, batch_of),
                  _const_spec(w_in.shape), _layer_spec(q_norm, j), _const_spec(w_q.shape),
                  _layer_spec(kv_norm, j), _layer_spec(w_kv, j), tab, tab],
        out_specs=[head_spec(QK_PAD), head_spec(MLA_NOPE), tab, head_spec(MLA_V)],
        out_shape=[jax.ShapeDtypeStruct((b, hq, s, QK_PAD), BF16),
                   jax.ShapeDtypeStruct((b, hq, s, MLA_NOPE), BF16),
                   jax.ShapeDtypeStruct((b, s, LANES), BF16),
                   jax.ShapeDtypeStruct((b, hq, s, MLA_V), BF16)],
        compiler_params=_params(("parallel", "parallel")),
        name="mla_pre",
    )(x, mod, mod, w_in, q_norm, w_q, kv_norm, w_kv, *tabs)


def _attn_kernel(q_ref, kn_ref, kr_ref, v_ref, o_ref, *, tq):
    s_len = q_ref.shape[2]
    for hd in range(q_ref.shape[1]):
        for i in range(s_len // tq):
            q = q_ref[0, hd, i * tq:(i + 1) * tq, :]
            m = jnp.full((tq, 1), -1e30, F32)
            l = jnp.zeros((tq, 1), F32)
            acc = jnp.zeros((tq, MLA_V), F32)
            for j in range(i + 1):
                rows = slice(j * tq, (j + 1) * tq)
                kb = jnp.concatenate([kn_ref[0, hd, rows, :], kr_ref[0, rows, :]], axis=1)
                vb = v_ref[0, hd, rows, :]
                s = lax.dot_general(q, kb, (((1,), (1,)), ((), ())),
                                    preferred_element_type=F32)
                if j == i:
                    r = lax.broadcasted_iota(jnp.int32, s.shape, 0) // CHUNK
                    c = lax.broadcasted_iota(jnp.int32, s.shape, 1) // CHUNK
                    s = jnp.where(c <= r, s, -jnp.inf)
                m_new = jnp.maximum(m, jnp.max(s, axis=-1, keepdims=True))
                alpha = jnp.exp2(m - m_new)
                p = jnp.exp2(s - m_new)
                l = alpha * l + jnp.sum(p, axis=-1, keepdims=True)
                acc = alpha * acc + jnp.dot(p.astype(BF16), vb, preferred_element_type=F32)
                m = m_new
            o_ref[0, i * tq:(i + 1) * tq, hd * MLA_V:(hd + 1) * MLA_V] = (
                acc / l).astype(o_ref.dtype)


def _attention(q, kn, kr, v):
    b, hq, s, _ = q.shape
    tq = 256
    hs = ATTN_HEADS_PER_STEP
    head_block = lambda w: pl.BlockSpec((1, hs, s, w), lambda bi, h: (bi, h, 0, 0))
    return pl.pallas_call(
        functools.partial(_attn_kernel, tq=tq),
        grid=(b, hq // hs),
        in_specs=[head_block(QK_PAD), head_block(MLA_NOPE),
                  pl.BlockSpec((1, s, LANES), lambda bi, h: (bi, 0, 0)), head_block(MLA_V)],
        out_specs=pl.BlockSpec((1, s, hs * MLA_V), lambda bi, h: (bi, 0, h)),
        out_shape=jax.ShapeDtypeStruct((b, s, hq * MLA_V), BF16),
        compiler_params=_params(("parallel", "parallel")),
        name="mla_attention",
    )(q, kn, kr, v)


def _log_sigmoid(x):
    return jnp.minimum(x, 0.0) - jnp.log1p(jnp.exp(-jnp.abs(x)))


def _ml_pre_kernel(x_ref, sc_ref, sh_ref, w_ref, wgt_ref, bgt_ref,
                   q_ref, k_ref, v_ref, og_ref, gr_ref, *, dk_all, dv_all):
    h = (_rms(x_ref[0]) * (1.0 + sc_ref[...]) + sh_ref[...]).astype(BF16)
    n_main = 2 * dk_all + 2 * dv_all
    proj = jnp.dot(h, w_ref[:, :n_main], preferred_element_type=F32)
    dk = dk_all // ML_HEADS
    q_ref[0] = proj[:, :dk_all].astype(BF16)
    k_ref[0] = (proj[:, dk_all:2 * dk_all] * dk ** -0.5).astype(BF16)
    v_ref[0] = proj[:, 2 * dk_all:2 * dk_all + dv_all].astype(BF16)
    og_ref[0] = proj[:, 2 * dk_all + dv_all:].astype(BF16)
    grow = lax.dot_general(wgt_ref[...], h, (((1,), (1,)), ((), ())),
                           preferred_element_type=F32) + bgt_ref[...]
    sub = lax.broadcasted_iota(jnp.int32, grow.shape, 0)
    gr_ref[0] = jnp.where(sub < ML_HEADS, grow, _log_sigmoid(grow))


def _ml_pre(x, mod, layer, w_in, w_gt, b_gt, j, dk_all, dv_all):
    b, s, d = x.shape
    tm = 1024
    row = lambda i, t: (i, t, 0)
    batch_of = lambda i, t: i
    ng = 2 * ML_HEADS
    return pl.pallas_call(
        functools.partial(_ml_pre_kernel, dk_all=dk_all, dv_all=dv_all),
        grid=(b, s // tm),
        in_specs=[pl.BlockSpec((1, tm, d), row),
                  _mod_spec(mod, layer, 1, batch_of), _mod_spec(mod, layer, 0, batch_of),
                  _layer_spec(w_in, j), _layer_spec(w_gt, j), _layer_spec(b_gt, j)],
        out_specs=[pl.BlockSpec((1, tm, dk_all), row), pl.BlockSpec((1, tm, dk_all), row),
                   pl.BlockSpec((1, tm, dv_all), row), pl.BlockSpec((1, tm, dv_all), row),
                   pl.BlockSpec((1, ng, tm), lambda i, t: (i, 0, t))],
        out_shape=[jax.ShapeDtypeStruct((b, s, dk_all), BF16),
                   jax.ShapeDtypeStruct((b, s, dk_all), BF16),
                   jax.ShapeDtypeStruct((b, s, dv_all), BF16),
                   jax.ShapeDtypeStruct((b, s, dv_all), BF16),
                   jax.ShapeDtypeStruct((b, ng, s), F32)],
        compiler_params=_params(("parallel", "parallel")),
        name="mlstm_pre",
    )(x, mod, mod, w_in, w_gt, b_gt)


def _ml_chunk_prep(gr):
    L = gr.shape[1]
    row = lax.broadcasted_iota(jnp.int32, (L, L), 0)
    col = lax.broadcasted_iota(jnp.int32, (L, L), 1)
    g2 = gr * LOG2E
    lane = lax.broadcasted_iota(jnp.int32, g2.shape, 1)
    cum = g2
    shift = 1
    while shift < L:
        cum = cum + jnp.where(lane >= shift, pltpu.roll(cum, shift, 1), 0.0)
        shift *= 2
    return col <= row, col == row, g2, cum


def _ml_head(h, prep, qh, kh, vh, ogh, hn, c_scr, n_scr, m_scr):
    causal, eye, g2, cum = prep
    L = qh.shape[0]
    i_row = g2[h:h + 1, :]
    lf_row = g2[ML_HEADS + h:ML_HEADS + h + 1, :]
    b_row = cum[ML_HEADS + h:ML_HEADS + h + 1, :]
    r_row = i_row - b_row
    m_prev = m_scr[h:h + 1, 0:1]
    n_row = n_scr[h:h + 1, :]

    r_mat = jnp.where(causal, r_row, -jnp.inf)
    mm = jnp.maximum(m_prev, jnp.max(r_mat, axis=-1, keepdims=True))
    b_col = jnp.sum(jnp.where(causal, lf_row, 0.0), axis=-1, keepdims=True)
    r_col = jnp.max(jnp.where(eye, r_row, -jnp.inf), axis=-1, keepdims=True)
    inter_w = jnp.exp2(m_prev - mm)
    s_mat = lax.dot_general(qh, kh, (((1,), (1,)), ((), ())),
                            preferred_element_type=F32) * jnp.exp2(r_mat - mm)
    c_state = c_scr[h]
    num = (inter_w * jnp.dot(qh, c_state.astype(BF16), preferred_element_type=F32)
           + jnp.dot(s_mat.astype(BF16), vh, preferred_element_type=F32))
    den = (inter_w * jnp.sum(qh.astype(F32) * n_row, axis=-1, keepdims=True)
           + jnp.sum(s_mat, axis=-1, keepdims=True))
    h_out = num * (1.0 / jnp.maximum(jnp.abs(den), jnp.exp2(-(b_col + mm))))
    y = (_rms(h_out) * hn * jax.nn.sigmoid(ogh.astype(F32))).astype(BF16)

    b_last = b_row[:, L - 1:L]
    w_log = b_last + r_row
    m_new = jnp.maximum(b_last + m_prev, jnp.max(w_log, axis=-1, keepdims=True))
    decay = jnp.exp2(b_last + m_prev - m_new)
    ws_col = jnp.exp2(r_col + (b_last - m_new))
    ws_row = jnp.exp2(w_log - m_new)
    wv = (ws_col * vh.astype(F32)).astype(BF16)
    c_scr[h] = decay * c_state + lax.dot_general(
        kh, wv, (((0,), (0,)), ((), ())), preferred_element_type=F32)
    ws8 = jnp.broadcast_to(ws_row, (SUBLANES, L)).astype(BF16)
    n_upd = jnp.dot(ws8, kh, preferred_element_type=F32)[0:1, :]
    n_scr[h:h + 1, :] = decay * n_row + n_upd
    m_scr[h:h + 1, :] = jnp.broadcast_to(m_new, (1, m_scr.shape[1]))
    return y


def _ml_state_shapes(dk, dv):
    return [pltpu.VMEM((ML_HEADS, dk, dv), F32), pltpu.VMEM((SUBLANES, dk), F32),
            pltpu.VMEM((SUBLANES, LANES), F32)]


def _ffn_tile(x, u, wo_ref, ga, sc, sh, gf, wup_ref, cw_ref, cb_ref, wdn_ref, a_scr, p_scr,
              interleave=()):
    tm, d = x.shape
    dff = p_scr.shape[1]
    halo = SUBLANES
    ck = FFN_CK
    interleave = dict(interleave)
    stage = [0]

    def emit_items():
        for item in interleave.pop(stage[0], ()):
            item()
        stage[0] += 1

    emit_items()
    y = jnp.dot(u, wo_ref[...], preferred_element_type=F32)
    x1 = x + ga * y
    h = (_rms(x1) * (1.0 + sc) + sh).astype(BF16)
    for c in range(dff // ck):
        lo = c * ck
        emit_items()
        a = jnp.dot(h, wup_ref[:, lo:lo + ck], preferred_element_type=F32)
        g = jnp.dot(h, wup_ref[:, dff + lo:dff + lo + ck], preferred_element_type=F32)
        a_scr[halo:halo + tm, lo:lo + ck] = a
        a1 = a_scr[halo - 1:halo - 1 + tm, lo:lo + ck]
        a2 = a_scr[halo - 2:halo - 2 + tm, lo:lo + ck]
        conv = (cw_ref[0:1, lo:lo + ck] * a2 + cw_ref[1:2, lo:lo + ck] * a1
                + cw_ref[2:3, lo:lo + ck] * a + cb_ref[:, lo:lo + ck])
        gelu = 0.5 * conv * (1.0 + lax.erf(conv * (2.0 ** -0.5)))
        p_scr[:, lo:lo + ck] = (gelu * g).astype(BF16)
        a_scr[0:halo, lo:lo + ck] = a_scr[tm:tm + halo, lo:lo + ck]
    cols = []
    for lo in range(0, d, ck):
        emit_items()
        f = jnp.dot(p_scr[...], wdn_ref[:, lo:lo + ck], preferred_element_type=F32)
        cols.append(x1[:, lo:lo + ck] + gf[:, lo:lo + ck] * f)
    assert not interleave, sorted(interleave)
    return jnp.concatenate(cols, axis=1)


def _ffn_scratch(tm, dff):
    return [pltpu.VMEM((tm + SUBLANES, dff), F32), pltpu.VMEM((tm, dff), BF16)]


def _ffn_specs(mod, layer, batch_of, w_out, j, ffn, final_norm):
    w_up, conv_w, conv_b, w_down = ffn
    return [_layer_spec(w_out, j), _mod_spec(mod, layer, 2, batch_of),
            _mod_spec(mod, layer, 4, batch_of), _mod_spec(mod, layer, 3, batch_of),
            _mod_spec(mod, layer, 5, batch_of), _layer_spec(w_up, layer),
            _layer_spec(conv_w, layer), _layer_spec(conv_b, layer), _layer_spec(w_down, layer),
            _const_spec(final_norm.shape)]


def _post_ffn_kernel(x_ref, u_ref, wo_ref, ga_ref, sc_ref, sh_ref, gf_ref, wup_ref, cw_ref,
                     cb_ref, wdn_ref, fn_ref, o_ref, a_scr, p_scr, *, final):
    @pl.when(pl.program_id(1) == 0)
    def _():
        a_scr[0:SUBLANES, :] = jnp.zeros((SUBLANES, a_scr.shape[1]), F32)

    x2 = _ffn_tile(x_ref[0], u_ref[0], wo_ref, ga_ref[...], sc_ref[...], sh_ref[...],
                   gf_ref[...], wup_ref, cw_ref, cb_ref, wdn_ref, a_scr, p_scr)
    if final:
        x2 = _rms(x2) * fn_ref[...]
    o_ref[0] = x2


def _post_ffn(x, u, mod, layer, w_out, j, ffn, final_norm, final):
    b, s, d = x.shape
    dff = ffn[3].shape[1]
    tm = FFN_TM
    row = lambda i, t: (i, t, 0)
    return pl.pallas_call(
        functools.partial(_post_ffn_kernel, final=final),
        grid=(b, s // tm),
        in_specs=([pl.BlockSpec((1, tm, d), row), pl.BlockSpec((1, tm, d), row)]
                  + _ffn_specs(mod, layer, lambda i, t: i, w_out, j, ffn, final_norm)),
        out_specs=pl.BlockSpec((1, tm, d), row),
        out_shape=jax.ShapeDtypeStruct((b, s, d), F32),
        scratch_shapes=_ffn_scratch(tm, dff),
        compiler_params=_params(("parallel", "arbitrary")),
        name="post_ffn",
    )(x, u, w_out, mod, mod, mod, mod, *ffn, final_norm)


def _ml_scan_ffn_kernel(x_ref, q_ref, k_ref, v_ref, og_ref, gr_ref, hn_ref, wo_ref,
                        ga_ref, sc_ref, sh_ref, gf_ref, wup_ref, cw_ref, cb_ref, wdn_ref, fn_ref,
                        o_ref, a_scr, p_scr, u_scr, c_scr, n_scr, m_scr,
                        *, final, nt, dk, dv):
    s = pl.program_id(0)
    tm = x_ref.shape[1]

    @pl.when(s == 0)
    def _():
        u_scr[...] = jnp.zeros_like(u_scr)

    @pl.when(s % nt == 0)
    def _():
        c_scr[...] = jnp.zeros_like(c_scr)
        n_scr[...] = jnp.zeros_like(n_scr)
        m_scr[...] = jnp.zeros_like(m_scr)

    @pl.when(jnp.maximum(s - 1, 0) % nt == 0)
    def _():
        a_scr[0:SUBLANES, :] = jnp.zeros((SUBLANES, a_scr.shape[1]), F32)

    L = ML_CHUNK
    preps = {}

    def prep_item(c):
        def run():
            preps[c] = _ml_chunk_prep(gr_ref[0, :, c * L:(c + 1) * L])
        return run

    def head_item(c, h):
        def run():
            r = slice(c * L, (c + 1) * L)
            kq = slice(h * dk, (h + 1) * dk)
            vo = slice(h * dv, (h + 1) * dv)
            u_scr[r, vo] = _ml_head(h, preps[c], q_ref[0, r, kq], k_ref[0, r, kq],
                                    v_ref[0, r, vo], og_ref[0, r, vo], hn_ref[:, vo],
                                    c_scr, n_scr, m_scr)
        return run

    items = []
    for c in range(tm // L):
        items += [prep_item(c)] + [head_item(c, h) for h in range(ML_HEADS)]
    n_stages = 1 + p_scr.shape[1] // FFN_CK + x_ref.shape[2] // FFN_CK
    interleave = {}
    for n, item in enumerate(items):
        interleave.setdefault(n * n_stages // len(items), []).append(item)
    u_prev = u_scr[...]
    x2 = _ffn_tile(x_ref[0], u_prev, wo_ref, ga_ref[...], sc_ref[...], sh_ref[...], gf_ref[...],
                   wup_ref, cw_ref, cb_ref, wdn_ref, a_scr, p_scr, interleave=interleave)
    if final:
        x2 = _rms(x2) * fn_ref[...]
    o_ref[0] = x2


def _ml_scan_ffn(x, q, k, v, og, gr, head_norm, mod, layer, w_out, j, ffn, final_norm, final):
    b, s, d = x.shape
    dff = ffn[3].shape[1]
    dk_all, dv_all = q.shape[-1], v.shape[-1]
    dk, dv = dk_all // ML_HEADS, dv_all // ML_HEADS
    tm = FFN_TM
    assert tm % ML_CHUNK == 0
    nt = s // tm
    last = b * nt - 1
    prod = lambda t: jnp.minimum(t, last)
    cons = lambda t: jnp.maximum(t - 1, 0)
    row_p = lambda t: (prod(t) // nt, prod(t) % nt, 0)
    row_c = lambda t: (cons(t) // nt, cons(t) % nt, 0)
    return pl.pallas_call(
        functools.partial(_ml_scan_ffn_kernel, final=final, nt=nt, dk=dk, dv=dv),
        grid=(b * nt + 1,),
        in_specs=([pl.BlockSpec((1, tm, d), row_c),
                   pl.BlockSpec((1, tm, dk_all), row_p), pl.BlockSpec((1, tm, dk_all), row_p),
                   pl.BlockSpec((1, tm, dv_all), row_p), pl.BlockSpec((1, tm, dv_all), row_p),
                   pl.BlockSpec((1, 2 * ML_HEADS, tm),
                                lambda t: (prod(t) // nt, 0, prod(t) % nt)),
                   _layer_spec(head_norm, j)]
                  + _ffn_specs(mod, layer, lambda t: cons(t) // nt, w_out, j, ffn, final_norm)),
        out_specs=pl.BlockSpec((1, tm, d), row_c),
        out_shape=jax.ShapeDtypeStruct((b, s, d), F32),
        scratch_shapes=(_ffn_scratch(tm, dff) + [pltpu.VMEM((tm, dv_all), BF16)]
                        + _ml_state_shapes(dk, dv)),
        compiler_params=_params(("arbitrary",)),
        name="mlstm_scan_ffn",
    )(x, q, k, v, og, gr, head_norm, w_out, mod, mod, mod, mod, *ffn, final_norm)


def kernel(x, c, positions, mod_w, mod_b, mla_w_in, mla_q_norm, mla_w_q_up, mla_kv_norm,
           mla_w_kv_up, mla_w_out, ml_w_in, ml_b_gates, ml_head_norm, ml_w_out, ffn_w_up,
           ffn_conv_w, ffn_conv_b, ffn_w_down, final_norm):
    b, s, d = x.shape
    depth = mod_w.shape[0]
    dff = ffn_w_down.shape[1]
    dv_all = ml_w_out.shape[1]
    ng = 2 * ML_HEADS
    dk_all = (ml_w_in.shape[2] - 2 * dv_all - ng) // 2
    n_main = 2 * dk_all + 2 * dv_all

    mod, tabs = _modulation_and_rope_tables(c, mod_w, mod_b, positions)
    mod = mod.reshape(depth, b, N_MOD, 1, d)
    fn = final_norm.reshape(1, d)

    ffn = (ffn_w_up.astype(BF16), ffn_conv_w, ffn_conv_b.reshape(depth, 1, dff),
           ffn_w_down.astype(BF16))
    n_in = mla_w_in.shape[2]
    mla_w_in_p = jnp.pad(mla_w_in, ((0, 0), (0, 0), (0, -n_in % LANES))).astype(BF16)
    na = mla_w_q_up.shape[0]
    w_q = mla_w_q_up.reshape(na, MLA_Q_LORA, MLA_HEADS, MLA_NOPE + MLA_ROPE)
    mla_w_q_p = jnp.concatenate([w_q[..., :MLA_NOPE].reshape(na, MLA_Q_LORA, -1),
                                 w_q[..., MLA_NOPE:].reshape(na, MLA_Q_LORA, -1)],
                                axis=-1).astype(BF16)
    mla_q_norm_r = mla_q_norm.reshape(na, 1, MLA_Q_LORA)
    mla_kv_norm_r = mla_kv_norm.reshape(na, 1, MLA_KV_LORA)
    mla_w_kv_b = mla_w_kv_up.astype(BF16)
    mla_w_out_b = mla_w_out.astype(BF16)
    nb = ml_w_in.shape[0]
    ml_w_in_b = ml_w_in.astype(BF16)
    ml_w_gt = jnp.swapaxes(ml_w_in[:, :, n_main:], 1, 2).astype(BF16)
    ml_b_gt = ml_b_gates.reshape(nb, ng, 1)
    ml_head_norm_r = ml_head_norm.reshape(nb, 1, dv_all)
    ml_w_out_b = ml_w_out.astype(BF16)

    for i in range(depth):
        j = i // 2
        final = i == depth - 1
        if i % 2 == 0:
            q, kn, kr, v = _mla_pre(x, mod, i, mla_w_in_p[j], mla_q_norm_r, mla_w_q_p[j],
                                    mla_kv_norm_r, mla_w_kv_b, j, tabs)
            u = _attention(q, kn, kr, v)
            x = _post_ffn(x, u, mod, i, mla_w_out_b, j, ffn, fn, final)
        else:
            q, k, v, og, gr = _ml_pre(x, mod, i, ml_w_in_b, ml_w_gt, ml_b_gt, j, dk_all, dv_all)
            x = _ml_scan_ffn(x, q, k, v, og, gr, ml_head_norm_r, mod, i, ml_w_out_b, j, ffn, fn,
                             final)
    return x
```

```python
import functools
import math

import jax
import jax.numpy as jnp
from jax import lax
from jax.experimental import pallas as pl
from jax.experimental.pallas import tpu as pltpu

EPS = 1e-6
ROPE_THETA = 10000.0
CHUNK = 64
MLA_HEADS = 8
MLA_Q_LORA = 512
MLA_KV_LORA = 256
MLA_NOPE = 128
MLA_ROPE = 64
MLA_V = 128
ML_HEADS = 4
N_MOD = 6

LANES = 128
SUBLANES = 8
QK_PAD = 256
ATTN_HEADS_PER_STEP = 4
ML_CHUNK = 256
FFN_TM = 512
FFN_CK = 256
VMEM_LIMIT = 56 * 1024 * 1024
LOG2E = math.log2(math.e)

BF16 = jnp.bfloat16
F32 = jnp.float32


def _params(sem):
    return pltpu.CompilerParams(dimension_semantics=sem, vmem_limit_bytes=VMEM_LIMIT)


def _const_spec(shape):
    nd = len(shape)
    return pl.BlockSpec(shape, lambda *_: (0,) * nd, pipeline_mode=pl.Buffered(1))


def _layer_spec(arr, layer):
    nd = arr.ndim - 1
    return pl.BlockSpec((None,) + arr.shape[1:], lambda *_: (layer,) + (0,) * nd,
                        pipeline_mode=pl.Buffered(1))


def _mod_spec(mod, layer, which, batch_of):
    d = mod.shape[-1]
    return pl.BlockSpec((None, None, None, 1, d),
                        lambda *ids: (layer, batch_of(*ids), which, 0, 0))


def _rms(x):
    return x * lax.rsqrt(jnp.mean(x * x, axis=-1, keepdims=True) + EPS)


def _mod_rope_kernel(c_ref, w_ref, b_ref, pos_ref, inv_ref, mod_ref, cos_ref, sin_ref):
    c = c_ref[...]
    ca = (c * jax.nn.sigmoid(c)).astype(BF16)
    mod_ref[0] = jnp.dot(ca, w_ref[0].astype(BF16), preferred_element_type=F32) + b_ref[0]

    half = MLA_ROPE // 2
    per_row = LANES // half
    pos4 = pos_ref[0].astype(F32)
    rows = pos4.shape[0]
    lane = lax.broadcasted_iota(jnp.int32, (rows, LANES), 1)
    pos = jnp.zeros((rows, LANES), F32)
    for g in range(per_row):
        pos = jnp.where(lane // half == g, pos4[:, g:g + 1], pos)
    ang = pos * inv_ref[...]
    cos, sin = jnp.cos(ang), jnp.sin(ang)

    def spread(x, g):
        x = pltpu.roll(x, LANES - half * g, 1) if g else x
        x = jnp.where(lane < half, x, 0.0)
        x = x + pltpu.roll(x, half, 1)
        return x + pltpu.roll(x, 2 * half, 1)

    for g in range(per_row):
        out_rows = pl.ds(g, rows, stride=per_row)
        cos_ref[0, out_rows, :] = spread(cos, g)
        sin_ref[0, out_rows, :] = spread(sin, g)


def _modulation_and_rope_tables(c, mod_w, mod_b, positions):
    depth, d, n = mod_w.shape
    b, s = positions.shape
    tn = 1536
    n_col = n // tn
    n_mod = depth * n_col
    half = MLA_ROPE // 2
    rows = s * half // LANES
    inv_freq = 1.0 / (ROPE_THETA ** (jnp.arange(0, MLA_ROPE, 2, dtype=F32) / MLA_ROPE))
    inv = jnp.tile(inv_freq, LANES // half).reshape(1, LANES)
    pos = positions.reshape(b, rows, LANES // half)
    mod_tile = lambda i: jnp.minimum(i, n_mod - 1)
    seq = lambda i: jnp.minimum(i, b - 1)
    w_idx = lambda i: (mod_tile(i) // n_col, 0, mod_tile(i) % n_col)
    tab = pl.BlockSpec((1, s, LANES), lambda i: (seq(i), 0, 0))
    tab_shape = jax.ShapeDtypeStruct((b, s, LANES), F32)
    mod, cos, sin = pl.pallas_call(
        _mod_rope_kernel,
        grid=(max(n_mod, b),),
        in_specs=[pl.BlockSpec((b, d), lambda i: (0, 0)),
                  pl.BlockSpec((1, d, tn), w_idx),
                  pl.BlockSpec((1, 1, tn), w_idx),
                  pl.BlockSpec((1, rows, LANES // half), lambda i: (seq(i), 0, 0)),
                  pl.BlockSpec((1, LANES), lambda i: (0, 0))],
        out_specs=[pl.BlockSpec((1, b, tn), w_idx), tab, tab],
        out_shape=[jax.ShapeDtypeStruct((depth, b, n), F32), tab_shape, tab_shape],
        compiler_params=_params(("arbitrary",)),
        name="modulation_rope",
    )(c, mod_w, mod_b.reshape(depth, 1, n), pos, inv)
    return mod, (cos, sin)


def _rope_coeffs(cos, sin):
    lane = lax.broadcasted_iota(jnp.int32, cos.shape, 1)
    first_half = (lane // (MLA_ROPE // 2)) % 2 == 0
    return cos, jnp.where(first_half, -sin, 0.0), jnp.where(first_half, 0.0, sin)


def _rope(g, ta, tb, tc):
    return (g * ta + pltpu.roll(g, LANES - MLA_ROPE // 2, 1) * tb
            + pltpu.roll(g, MLA_ROPE // 2, 1) * tc)


def _mla_pre_kernel(x_ref, sc_ref, sh_ref, win_ref, qn_ref, wq_ref, kvn_ref, wkv_ref,
                    cos_ref, sin_ref, q_ref, kn_ref, kr_ref, v_ref):
    h = _rms(x_ref[0]) * (1.0 + sc_ref[...]) + sh_ref[...]
    proj = jnp.dot(h.astype(BF16), win_ref[...], preferred_element_type=F32)
    cq = proj[:, :MLA_Q_LORA]
    ckv = proj[:, MLA_Q_LORA:MLA_Q_LORA + MLA_KV_LORA]
    kr = proj[:, MLA_Q_LORA + MLA_KV_LORA:]
    ta, tb, tc = _rope_coeffs(cos_ref[0], sin_ref[0])
    qa = jnp.dot((_rms(cq) * qn_ref[...]).astype(BF16), wq_ref[...],
                 preferred_element_type=F32)
    kva = jnp.dot((_rms(ckv) * kvn_ref[...]).astype(BF16), wkv_ref[...],
                  preferred_element_type=F32)
    kr_ref[0] = _rope(kr, ta, tb, tc).astype(BF16)
    scale = (MLA_NOPE + MLA_ROPE) ** -0.5 * LOG2E
    lane = lax.broadcasted_iota(jnp.int32, kr.shape, 1)
    n_nope = MLA_HEADS * MLA_NOPE
    for pair in range(MLA_HEADS // 2):
        lo = n_nope + pair * LANES
        rp = _rope(qa[:, lo:lo + LANES], ta, tb, tc) * scale
        for hd, part in ((2 * pair, rp), (2 * pair + 1, pltpu.roll(rp, MLA_ROPE, 1))):
            q_ref[0, hd, :, MLA_NOPE:] = jnp.where(lane < MLA_ROPE, part, 0.0).astype(BF16)
    for hd in range(MLA_HEADS):
        o = hd * QK_PAD
        q_ref[0, hd, :, :MLA_NOPE] = (qa[:, hd * MLA_NOPE:(hd + 1) * MLA_NOPE]
                                      * scale).astype(BF16)
        kn_ref[0, hd] = kva[:, o:o + MLA_NOPE].astype(BF16)
        v_ref[0, hd] = kva[:, o + MLA_NOPE:o + MLA_NOPE + MLA_V].astype(BF16)


def _mla_pre(x, mod, layer, w_in, q_norm, w_q, kv_norm, w_kv, j, tabs):
    b, s, d = x.shape
    tm = 1024
    hq = MLA_HEADS
    row = lambda i, t: (i, t, 0)
    batch_of = lambda i, t: i
    tab = pl.BlockSpec((1, tm, LANES), row)
    head_spec = lambda w: pl.BlockSpec((1, hq, tm, w), lambda i, t: (i, 0, t, 0))
    return pl.pallas_call(
        _mla_pre_kernel,
        grid=(b, s // tm),
        in_specs=[pl.BlockSpec((1, tm, d), row),
                  _mod_spec(mod, layer, 1, batch_of), _mod_spec(mod, layer, 0, batch_of),
                  _const_spec(w_in.shape), _layer_spec(q_norm, j), _const_spec(w_q.shape),
                  _layer_spec(kv_norm, j), _layer_spec(w_kv, j), tab, tab],
        out_specs=[head_spec(QK_PAD), head_spec(MLA_NOPE), tab, head_spec(MLA_V)],
        out_shape=[jax.ShapeDtypeStruct((b, hq, s, QK_PAD), BF16),
                   jax.ShapeDtypeStruct((b, hq, s, MLA_NOPE), BF16),
                   jax.ShapeDtypeStruct((b, s, LANES), BF16),
                   jax.ShapeDtypeStruct((b, hq, s, MLA_V), BF16)],
        compiler_params=_params(("parallel", "parallel")),
        name="mla_pre",
    )(x, mod, mod, w_in, q_norm, w_q, kv_norm, w_kv, *tabs)


def _attn_kernel(q_ref, kn_ref, kr_ref, v_ref, o_ref, *, tq):
    s_len = q_ref.shape[2]
    for hd in range(q_ref.shape[1]):
        for i in range(s_len // tq):
            q = q_ref[0, hd, i * tq:(i + 1) * tq, :]
            m = jnp.full((tq, 1), -1e30, F32)
            l = jnp.zeros((tq, 1), F32)
            acc = jnp.zeros((tq, MLA_V), F32)
            for j in range(i + 1):
                rows = slice(j * tq, (j + 1) * tq)
                kb = jnp.concatenate([kn_ref[0, hd, rows, :], kr_ref[0, rows, :]], axis=1)
                vb = v_ref[0, hd, rows, :]
                s = lax.dot_general(q, kb, (((1,), (1,)), ((), ())),
                                    preferred_element_type=F32)
                if j == i:
                    r = lax.broadcasted_iota(jnp.int32, s.shape, 0) // CHUNK
                    c = lax.broadcasted_iota(jnp.int32, s.shape, 1) // CHUNK
                    s = jnp.where(c <= r, s, -jnp.inf)
                m_new = jnp.maximum(m, jnp.max(s, axis=-1, keepdims=True))
                alpha = jnp.exp2(m - m_new)
                p = jnp.exp2(s - m_new)
                l = alpha * l + jnp.sum(p, axis=-1, keepdims=True)
                acc = alpha * acc + jnp.dot(p.astype(BF16), vb, preferred_element_type=F32)
                m = m_new
            o_ref[0, i * tq:(i + 1) * tq, hd * MLA_V:(hd + 1) * MLA_V] = (
                acc / l).astype(o_ref.dtype)


def _attention(q, kn, kr, v):
    b, hq, s, _ = q.shape
    tq = 256
    hs = ATTN_HEADS_PER_STEP
    head_block = lambda w: pl.BlockSpec((1, hs, s, w), lambda bi, h: (bi, h, 0, 0))
    return pl.pallas_call(
        functools.partial(_attn_kernel, tq=tq),
        grid=(b, hq // hs),
        in_specs=[head_block(QK_PAD), head_block(MLA_NOPE),
                  pl.BlockSpec((1, s, LANES), lambda bi, h: (bi, 0, 0)), head_block(MLA_V)],
        out_specs=pl.BlockSpec((1, s, hs * MLA_V), lambda bi, h: (bi, 0, h)),
        out_shape=jax.ShapeDtypeStruct((b, s, hq * MLA_V), BF16),
        compiler_params=_params(("parallel", "parallel")),
        name="mla_attention",
    )(q, kn, kr, v)


def _log_sigmoid(x):
    return jnp.minimum(x, 0.0) - jnp.log1p(jnp.exp(-jnp.abs(x)))


def _ml_pre_kernel(x_ref, sc_ref, sh_ref, w_ref, wgt_ref, bgt_ref,
                   q_ref, k_ref, v_ref, og_ref, gr_ref, *, dk_all, dv_all):
    h = (_rms(x_ref[0]) * (1.0 + sc_ref[...]) + sh_ref[...]).astype(BF16)
    n_main = 2 * dk_all + 2 * dv_all
    proj = jnp.dot(h, w_ref[:, :n_main], preferred_element_type=F32)
    dk = dk_all // ML_HEADS
    q_ref[0] = proj[:, :dk_all].astype(BF16)
    k_ref[0] = (proj[:, dk_all:2 * dk_all] * dk ** -0.5).astype(BF16)
    v_ref[0] = proj[:, 2 * dk_all:2 * dk_all + dv_all].astype(BF16)
    og_ref[0] = proj[:, 2 * dk_all + dv_all:].astype(BF16)
    grow = lax.dot_general(wgt_ref[...], h, (((1,), (1,)), ((), ())),
                           preferred_element_type=F32) + bgt_ref[...]
    sub = lax.broadcasted_iota(jnp.int32, grow.shape, 0)
    gr_ref[0] = jnp.where(sub < ML_HEADS, grow, _log_sigmoid(grow))


def _ml_pre(x, mod, layer, w_in, w_gt, b_gt, j, dk_all, dv_all):
    b, s, d = x.shape
    tm = 1024
    row = lambda i, t: (i, t, 0)
    batch_of = lambda i, t: i
    ng = 2 * ML_HEADS
    return pl.pallas_call(
        functools.partial(_ml_pre_kernel, dk_all=dk_all, dv_all=dv_all),
        grid=(b, s // tm),
        in_specs=[pl.BlockSpec((1, tm, d), row),
                  _mod_spec(mod, layer, 1, batch_of), _mod_spec(mod, layer, 0, batch_of),
                  _layer_spec(w_in, j), _layer_spec(w_gt, j), _layer_spec(b_gt, j)],
        out_specs=[pl.BlockSpec((1, tm, dk_all), row), pl.BlockSpec((1, tm, dk_all), row),
                   pl.BlockSpec((1, tm, dv_all), row), pl.BlockSpec((1, tm, dv_all), row),
                   pl.BlockSpec((1, ng, tm), lambda i, t: (i, 0, t))],
        out_shape=[jax.ShapeDtypeStruct((b, s, dk_all), BF16),
                   jax.ShapeDtypeStruct((b, s, dk_all), BF16),
                   jax.ShapeDtypeStruct((b, s, dv_all), BF16),
                   jax.ShapeDtypeStruct((b, s, dv_all), BF16),
                   jax.ShapeDtypeStruct((b, ng, s), F32)],
        compiler_params=_params(("parallel", "parallel")),
        name="mlstm_pre",
    )(x, mod, mod, w_in, w_gt, b_gt)


def _ml_chunk_prep(gr):
    L = gr.shape[1]
    row = lax.broadcasted_iota(jnp.int32, (L, L), 0)
    col = lax.broadcasted_iota(jnp.int32, (L, L), 1)
    g2 = gr * LOG2E
    lane = lax.broadcasted_iota(jnp.int32, g2.shape, 1)
    cum = g2
    shift = 1
    while shift < L:
        cum = cum + jnp.where(lane >= shift, pltpu.roll(cum, shift, 1), 0.0)
        shift *= 2
    return col <= row, col == row, g2, cum


def _ml_head(h, prep, qh, kh, vh, ogh, hn, c_scr, n_scr, m_scr):
    causal, eye, g2, cum = prep
    L = qh.shape[0]
    i_row = g2[h:h + 1, :]
    lf_row = g2[ML_HEADS + h:ML_HEADS + h + 1, :]
    b_row = cum[ML_HEADS + h:ML_HEADS + h + 1, :]
    r_row = i_row - b_row
    m_prev = m_scr[h:h + 1, 0:1]
    n_row = n_scr[h:h + 1, :]

    r_mat = jnp.where(causal, r_row, -jnp.inf)
    mm = jnp.maximum(m_prev, jnp.max(r_mat, axis=-1, keepdims=True))
    b_col = jnp.sum(jnp.where(causal, lf_row, 0.0), axis=-1, keepdims=True)
    r_col = jnp.max(jnp.where(eye, r_row, -jnp.inf), axis=-1, keepdims=True)
    inter_w = jnp.exp2(m_prev - mm)
    s_mat = lax.dot_general(qh, kh, (((1,), (1,)), ((), ())),
                            preferred_element_type=F32) * jnp.exp2(r_mat - mm)
    c_state = c_scr[h]
    num = (inter_w * jnp.dot(qh, c_state.astype(BF16), preferred_element_type=F32)
           + jnp.dot(s_mat.astype(BF16), vh, preferred_element_type=F32))
    den = (inter_w * jnp.sum(qh.astype(F32) * n_row, axis=-1, keepdims=True)
           + jnp.sum(s_mat, axis=-1, keepdims=True))
    h_out = num * (1.0 / jnp.maximum(jnp.abs(den), jnp.exp2(-(b_col + mm))))
    y = (_rms(h_out) * hn * jax.nn.sigmoid(ogh.astype(F32))).astype(BF16)

    b_last = b_row[:, L - 1:L]
    w_log = b_last + r_row
    m_new = jnp.maximum(b_last + m_prev, jnp.max(w_log, axis=-1, keepdims=True))
    decay = jnp.exp2(b_last + m_prev - m_new)
    ws_col = jnp.exp2(r_col + (b_last - m_new))
    ws_row = jnp.exp2(w_log - m_new)
    wv = (ws_col * vh.astype(F32)).astype(BF16)
    c_scr[h] = decay * c_state + lax.dot_general(
        kh, wv, (((0,), (0,)), ((), ())), preferred_element_type=F32)
    ws8 = jnp.broadcast_to(ws_row, (SUBLANES, L)).astype(BF16)
    n_upd = jnp.dot(ws8, kh, preferred_element_type=F32)[0:1, :]
    n_scr[h:h + 1, :] = decay * n_row + n_upd
    m_scr[h:h + 1, :] = jnp.broadcast_to(m_new, (1, m_scr.shape[1]))
    return y


def _ml_state_shapes(dk, dv):
    return [pltpu.VMEM((ML_HEADS, dk, dv), F32), pltpu.VMEM((SUBLANES, dk), F32),
            pltpu.VMEM((SUBLANES, LANES), F32)]


def _ffn_tile(x, u, wo_ref, ga, sc, sh, gf, wup_ref, cw_ref, cb_ref, wdn_ref, a_scr, p_scr,
              interleave=()):
    tm, d = x.shape
    dff = p_scr.shape[1]
    halo = SUBLANES
    ck = FFN_CK
    interleave = dict(interleave)
    stage = [0]

    def emit_items():
        for item in interleave.pop(stage[0], ()):
            item()
        stage[0] += 1

    emit_items()
    y = jnp.dot(u, wo_ref[...], preferred_element_type=F32)
    x1 = x + ga * y
    h = (_rms(x1) * (1.0 + sc) + sh).astype(BF16)
    for c in range(dff // ck):
        lo = c * ck
        emit_items()
        a = jnp.dot(h, wup_ref[:, lo:lo + ck], preferred_element_type=F32)
        g = jnp.dot(h, wup_ref[:, dff + lo:dff + lo + ck], preferred_element_type=F32)
        a_scr[halo:halo + tm, lo:lo + ck] = a
        a1 = a_scr[halo - 1:halo - 1 + tm, lo:lo + ck]
        a2 = a_scr[halo - 2:halo - 2 + tm, lo:lo + ck]
        conv = (cw_ref[0:1, lo:lo + ck] * a2 + cw_ref[1:2, lo:lo + ck] * a1
                + cw_ref[2:3, lo:lo + ck] * a + cb_ref[:, lo:lo + ck])
        gelu = 0.5 * conv * (1.0 + lax.erf(conv * (2.0 ** -0.5)))
        p_scr[:, lo:lo + ck] = (gelu * g).astype(BF16)
        a_scr[0:halo, lo:lo + ck] = a_scr[tm:tm + halo, lo:lo + ck]
    cols = []
    for lo in range(0, d, ck):
        emit_items()
        f = jnp.dot(p_scr[...], wdn_ref[:, lo:lo + ck], preferred_element_type=F32)
        cols.append(x1[:, lo:lo + ck] + gf[:, lo:lo + ck] * f)
    assert not interleave, sorted(interleave)
    return jnp.concatenate(cols, axis=1)


def _ffn_scratch(tm, dff):
    return [pltpu.VMEM((tm + SUBLANES, dff), F32), pltpu.VMEM((tm, dff), BF16)]


def _ffn_specs(mod, layer, batch_of, w_out, j, ffn, final_norm):
    w_up, conv_w, conv_b, w_down = ffn
    return [_layer_spec(w_out, j), _mod_spec(mod, layer, 2, batch_of),
            _mod_spec(mod, layer, 4, batch_of), _mod_spec(mod, layer, 3, batch_of),
            _mod_spec(mod, layer, 5, batch_of), _layer_spec(w_up, layer),
            _layer_spec(conv_w, layer), _layer_spec(conv_b, layer), _layer_spec(w_down, layer),
            _const_spec(final_norm.shape)]


def _post_ffn_kernel(x_ref, u_ref, wo_ref, ga_ref, sc_ref, sh_ref, gf_ref, wup_ref, cw_ref,
                     cb_ref, wdn_ref, fn_ref, o_ref, a_scr, p_scr, *, final):
    @pl.when(pl.program_id(1) == 0)
    def _():
        a_scr[0:SUBLANES, :] = jnp.zeros((SUBLANES, a_scr.shape[1]), F32)

    x2 = _ffn_tile(x_ref[0], u_ref[0], wo_ref, ga_ref[...], sc_ref[...], sh_ref[...],
                   gf_ref[...], wup_ref, cw_ref, cb_ref, wdn_ref, a_scr, p_scr)
    if final:
        x2 = _rms(x2) * fn_ref[...]
    o_ref[0] = x2


def _post_ffn(x, u, mod, layer, w_out, j, ffn, final_norm, final):
    b, s, d = x.shape
    dff = ffn[3].shape[1]
    tm = FFN_TM
    row = lambda i, t: (i, t, 0)
    return pl.pallas_call(
        functools.partial(_post_ffn_kernel, final=final),
        grid=(b, s // tm),
        in_specs=([pl.BlockSpec((1, tm, d), row), pl.BlockSpec((1, tm, d), row)]
                  + _ffn_specs(mod, layer, lambda i, t: i, w_out, j, ffn, final_norm)),
        out_specs=pl.BlockSpec((1, tm, d), row),
        out_shape=jax.ShapeDtypeStruct((b, s, d), F32),
        scratch_shapes=_ffn_scratch(tm, dff),
        compiler_params=_params(("parallel", "arbitrary")),
        name="post_ffn",
    )(x, u, w_out, mod, mod, mod, mod, *ffn, final_norm)


def _ml_scan_ffn_kernel(x_ref, q_ref, k_ref, v_ref, og_ref, gr_ref, hn_ref, wo_ref,
                        ga_ref, sc_ref, sh_ref, gf_ref, wup_ref, cw_ref, cb_ref, wdn_ref, fn_ref,
                        o_ref, a_scr, p_scr, u_scr, c_scr, n_scr, m_scr,
                        *, final, nt, dk, dv):
    s = pl.program_id(0)
    tm = x_ref.shape[1]

    @pl.when(s == 0)
    def _():
        u_scr[...] = jnp.zeros_like(u_scr)

    @pl.when(s % nt == 0)
    def _():
        c_scr[...] = jnp.zeros_like(c_scr)
        n_scr[...] = jnp.zeros_like(n_scr)
        m_scr[...] = jnp.zeros_like(m_scr)

    @pl.when(jnp.maximum(s - 1, 0) % nt == 0)
    def _():
        a_scr[0:SUBLANES, :] = jnp.zeros((SUBLANES, a_scr.shape[1]), F32)

    L = ML_CHUNK
    preps = {}

    def prep_item(c):
        def run():
            preps[c] = _ml_chunk_prep(gr_ref[0, :, c * L:(c + 1) * L])
        return run

    def head_item(c, h):
        def run():
            r = slice(c * L, (c + 1) * L)
            kq = slice(h * dk, (h + 1) * dk)
            vo = slice(h * dv, (h + 1) * dv)
            u_scr[r, vo] = _ml_head(h, preps[c], q_ref[0, r, kq], k_ref[0, r, kq],
                                    v_ref[0, r, vo], og_ref[0, r, vo], hn_ref[:, vo],
                                    c_scr, n_scr, m_scr)
        return run

    items = []
    for c in range(tm // L):
        items += [prep_item(c)] + [head_item(c, h) for h in range(ML_HEADS)]
    n_stages = 1 + p_scr.shape[1] // FFN_CK + x_ref.shape[2] // FFN_CK
    interleave = {}
    for n, item in enumerate(items):
        interleave.setdefault(n * n_stages // len(items), []).append(item)
    u_prev = u_scr[...]
    x2 = _ffn_tile(x_ref[0], u_prev, wo_ref, ga_ref[...], sc_ref[...], sh_ref[...], gf_ref[...],
                   wup_ref, cw_ref, cb_ref, wdn_ref, a_scr, p_scr, interleave=interleave)
    if final:
        x2 = _rms(x2) * fn_ref[...]
    o_ref[0] = x2


def _ml_scan_ffn(x, q, k, v, og, gr, head_norm, mod, layer, w_out, j, ffn, final_norm, final):
    b, s, d = x.shape
    dff = ffn[3].shape[1]
    dk_all, dv_all = q.shape[-1], v.shape[-1]
    dk, dv = dk_all // ML_HEADS, dv_all // ML_HEADS
    tm = FFN_TM
    assert tm % ML_CHUNK == 0
    nt = s // tm
    last = b * nt - 1
    prod = lambda t: jnp.minimum(t, last)
    cons = lambda t: jnp.maximum(t - 1, 0)
    row_p = lambda t: (prod(t) // nt, prod(t) % nt, 0)
    row_c = lambda t: (cons(t) // nt, cons(t) % nt, 0)
    return pl.pallas_call(
        functools.partial(_ml_scan_ffn_kernel, final=final, nt=nt, dk=dk, dv=dv),
        grid=(b * nt + 1,),
        in_specs=([pl.BlockSpec((1, tm, d), row_c),
                   pl.BlockSpec((1, tm, dk_all), row_p), pl.BlockSpec((1, tm, dk_all), row_p),
                   pl.BlockSpec((1, tm, dv_all), row_p), pl.BlockSpec((1, tm, dv_all), row_p),
                   pl.BlockSpec((1, 2 * ML_HEADS, tm),
                                lambda t: (prod(t) // nt, 0, prod(t) % nt)),
                   _layer_spec(head_norm, j)]
                  + _ffn_specs(mod, layer, lambda t: cons(t) // nt, w_out, j, ffn, final_norm)),
        out_specs=pl.BlockSpec((1, tm, d), row_c),
        out_shape=jax.ShapeDtypeStruct((b, s, d), F32),
        scratch_shapes=(_ffn_scratch(tm, dff) + [pltpu.VMEM((tm, dv_all), BF16)]
                        + _ml_state_shapes(dk, dv)),
        compiler_params=_params(("arbitrary",)),
        name="mlstm_scan_ffn",
    )(x, q, k, v, og, gr, head_norm, w_out, mod, mod, mod, mod, *ffn, final_norm)


def kernel(x, c, positions, mod_w, mod_b, mla_w_in, mla_q_norm, mla_w_q_up, mla_kv_norm,
           mla_w_kv_up, mla_w_out, ml_w_in, ml_b_gates, ml_head_norm, ml_w_out, ffn_w_up,
           ffn_conv_w, ffn_conv_b, ffn_w_down, final_norm):
    b, s, d = x.shape
    depth = mod_w.shape[0]
    dff = ffn_w_down.shape[1]
    dv_all = ml_w_out.shape[1]
    ng = 2 * ML_HEADS
    dk_all = (ml_w_in.shape[2] - 2 * dv_all - ng) // 2
    n_main = 2 * dk_all + 2 * dv_all

    mod, tabs = _modulation_and_rope_tables(c, mod_w, mod_b, positions)
    mod = mod.reshape(depth, b, N_MOD, 1, d)
    fn = final_norm.reshape(1, d)

    ffn = (ffn_w_up.astype(BF16), ffn_conv_w, ffn_conv_b.reshape(depth, 1, dff),
           ffn_w_down.astype(BF16))
    n_in = mla_w_in.shape[2]
    mla_w_in_p = jnp.pad(mla_w_in, ((0, 0), (0, 0), (0, -n_in % LANES))).astype(BF16)
    na = mla_w_q_up.shape[0]
    w_q = mla_w_q_up.reshape(na, MLA_Q_LORA, MLA_HEADS, MLA_NOPE + MLA_ROPE)
    mla_w_q_p = jnp.concatenate([w_q[..., :MLA_NOPE].reshape(na, MLA_Q_LORA, -1),
                                 w_q[..., MLA_NOPE:].reshape(na, MLA_Q_LORA, -1)],
                                axis=-1).astype(BF16)
    mla_q_norm_r = mla_q_norm.reshape(na, 1, MLA_Q_LORA)
    mla_kv_norm_r = mla_kv_norm.reshape(na, 1, MLA_KV_LORA)
    mla_w_kv_b = mla_w_kv_up.astype(BF16)
    mla_w_out_b = mla_w_out.astype(BF16)
    nb = ml_w_in.shape[0]
    ml_w_in_b = ml_w_in.astype(BF16)
    ml_w_gt = jnp.swapaxes(ml_w_in[:, :, n_main:], 1, 2).astype(BF16)
    ml_b_gt = ml_b_gates.reshape(nb, ng, 1)
    ml_head_norm_r = ml_head_norm.reshape(nb, 1, dv_all)
    ml_w_out_b = ml_w_out.astype(BF16)

    for i in range(depth):
        j = i // 2
        final = i == depth - 1
        if i % 2 == 0:
            q, kn, kr, v = _mla_pre(x, mod, i, mla_w_in_p[j], mla_q_norm_r, mla_w_q_p[j],
                                    mla_kv_norm_r, mla_w_kv_b, j, tabs)
            u = _attention(q, kn, kr, v)
            x = _post_ffn(x, u, mod, i, mla_w_out_b, j, ffn, fn, final)
        else:
            q, k, v, og, gr = _ml_pre(x, mod, i, ml_w_in_b, ml_w_gt, ml_b_gt, j, dk_all, dv_all)
            x = _ml_scan_ffn(x, q, k, v, og, gr, ml_head_norm_r, mod, i, ml_w_out_b, j, ffn, fn,
                             final)
    return x
```
